```python
import jax, jax.numpy as jnp
from jax import lax
import numpy as np

D_MODEL = 2048
BATCH = 1
SEQ = 16384
DEPTH = 1

GRID_W = 64
ATT_HEADS = 8
ATT_KV_HEADS = 2
HEAD_DIM = 128
ATT_WIDTH = ATT_HEADS * HEAD_DIM
KV_WIDTH = ATT_KV_HEADS * HEAD_DIM
AXIS_DIM = HEAD_DIM // 2
ROPE_THETA = 10000.0
Q_BLOCK = 128
HG_WIDTH = D_MODEL - ATT_WIDTH
HG_HEADS = 8
HG_HEAD_V = HG_WIDTH // HG_HEADS
HG_EXPAND = 128
HG_FORGET = HG_HEADS * HG_EXPAND
CHUNK = 64
IN_SIZES = (ATT_WIDTH, KV_WIDTH, KV_WIDTH,
            HG_FORGET, HG_FORGET, HG_FORGET,
            HG_WIDTH, HG_WIDTH)
IN_COLS = sum(IN_SIZES)
MIX_WIDTH = ATT_WIDTH + HG_WIDTH
D_FF = 5632
CONV_W = 3
EPS = 1e-6

kernel_name = 'hybrid_attn_hgrn2_convffn_encoder'


def rms_norm(x, w):
    xf = x.astype(jnp.float32)
    y = xf * lax.rsqrt(jnp.mean(xf * xf, axis=-1, keepdims=True) + EPS)
    return (y * w.astype(jnp.float32)).astype(x.dtype)


def _rotate_half(x):
    x1, x2 = jnp.split(x, 2, axis=-1)
    return jnp.concatenate([-x2, x1], axis=-1)


def axial_rope_tables(S):
    rows = S // GRID_W
    t = jnp.arange(S)
    r = (t // GRID_W - rows // 2).astype(jnp.float32)
    cpos = (t % GRID_W - GRID_W // 2).astype(jnp.float32)
    inv = ROPE_THETA ** (-(2.0 * jnp.arange(AXIS_DIM // 2, dtype=jnp.float32)) / AXIS_DIM)
    ang_r = r[:, None] * inv[None, :]
    ang_c = cpos[:, None] * inv[None, :]
    ang = jnp.concatenate([ang_r, ang_r, ang_c, ang_c], axis=-1)
    return jnp.cos(ang), jnp.sin(ang)


def apply_axial_rope(x, cos, sin):
    xf = x.astype(jnp.float32)
    xr, xc = jnp.split(xf, 2, axis=-1)
    rot = jnp.concatenate([_rotate_half(xr), _rotate_half(xc)], axis=-1)
    y = xf * cos[None, :, None, :] + rot * sin[None, :, None, :]
    return y.astype(x.dtype)


def attention_group(q, k, v, q_norm_w, k_norm_w):
    B, S, _ = q.shape
    G = ATT_HEADS // ATT_KV_HEADS
    q = rms_norm(q.reshape(B, S, ATT_HEADS, HEAD_DIM), q_norm_w)
    k = rms_norm(k.reshape(B, S, ATT_KV_HEADS, HEAD_DIM), k_norm_w)
    v = v.reshape(B, S, ATT_KV_HEADS, HEAD_DIM)
    cos, sin = axial_rope_tables(S)
    q = apply_axial_rope(q, cos, sin) * (HEAD_DIM ** -0.5)
    k = apply_axial_rope(k, cos, sin)
    nb = S // Q_BLOCK
    qb = q.reshape(B, nb, Q_BLOCK, ATT_KV_HEADS, G, HEAD_DIM).transpose(1, 0, 2, 3, 4, 5)

    def block(q_blk):
        s = jnp.einsum('bqhgd,bkhd->bhgqk', q_blk, k).astype(jnp.float32)
        p = jax.nn.softmax(s, axis=-1).astype(v.dtype)
        return jnp.einsum('bhgqk,bkhd->bqhgd', p, v)

    o = lax.map(block, qb)
    return o.transpose(1, 0, 2, 3, 4, 5).reshape(B, S, ATT_WIDTH)


def gla_chunk_scan(q, k, v, g):
    _, ND, B, H, C, K = q.shape
    V = v.shape[-1]
    mask = jnp.tril(jnp.ones((C, C), dtype=bool))[:, :, None]

    def step(state, inp):
        qc, kc, vc, gc = inp
        b = jnp.cumsum(gc, axis=-2)
        o_inter = jnp.einsum('dbhck,dbhkv->dbhcv', qc * jnp.exp(b), state)
        diff = b[..., :, None, :] - b[..., None, :, :]
        decay = jnp.exp(jnp.where(mask, diff, -jnp.inf))
        a = jnp.einsum('dbhck,dbhcsk,dbhsk->dbhcs', qc, decay, kc)
        o_intra = jnp.einsum('dbhcs,dbhsv->dbhcv', a, vc)
        b_last = b[..., -1:, :]
        new_state = (jnp.exp(b_last[..., 0, :])[..., None] * state
                     + jnp.einsum('dbhck,dbhcv->dbhkv', kc * jnp.exp(b_last - b), vc))
        return new_state, o_inter + o_intra

    init = jnp.zeros((ND, B, H, K, V), jnp.float32)
    _, o = lax.scan(step, init, (q, k, v, g))
    return o


def hgrn2_group(q, f_fwd, f_bwd, i, gate, lb_param, out_norm_w, layer):
    B, S, _ = q.shape
    N = S // CHUNK
    lb = jnp.cumsum(jax.nn.softmax(lb_param.astype(jnp.float32), axis=0), axis=0)[layer]
    lbx = lb[:, None, None, :]
    qf = jax.nn.silu(q.astype(jnp.float32)) * (HG_EXPAND ** -0.5)
    f = lbx + (1.0 - lbx) * jax.nn.sigmoid(jnp.stack([f_fwd, f_bwd], 0).astype(jnp.float32))
    kk = 1.0 - f
    g = jnp.log(f)
    vf = i.astype(jnp.float32)
    flip = lambda a: jnp.flip(a, axis=1)
    qd = jnp.stack([qf, flip(qf)])
    vd = jnp.stack([vf, flip(vf)])
    kd = jnp.stack([kk[0], flip(kk[1])])
    gd = jnp.stack([g[0], flip(g[1])])

    def chunked(a, dh):
        return a.reshape(2, B, N, CHUNK, HG_HEADS, dh).transpose(2, 0, 1, 4, 3, 5)

    o = gla_chunk_scan(chunked(qd, HG_EXPAND), chunked(kd, HG_EXPAND),
                       chunked(vd, HG_HEAD_V), chunked(gd, HG_EXPAND))
    o = o.transpose(1, 2, 0, 4, 3, 5).reshape(2, B, S, HG_HEADS, HG_HEAD_V)
    o = o[0] + flip(o[1])
    o = rms_norm(o, out_norm_w) * jax.nn.silu(gate.astype(jnp.float32).reshape(B, S, HG_HEADS, HG_HEAD_V))
    return o.reshape(B, S, HG_WIDTH).astype(i.dtype)


def conv_glu_ffn(h, w_up, conv_w, conv_b, w_down):
    u = h @ w_up
    C = u.shape[-1]
    u = lax.conv_general_dilated(u, conv_w[:, None, :], window_strides=(1,), padding='SAME',
                                 dimension_numbers=('NWC', 'WIO', 'NWC'),
                                 feature_group_count=C) + conv_b
    a, b = jnp.split(u, 2, axis=-1)
    return (jax.nn.gelu(a, approximate=True) * b) @ w_down


def setup_inputs(seed: int = 0) -> dict:
    key = jax.random.key(seed)
    ks = jax.random.split(key, 20)
    nrm = lambda k, shape, s: jax.random.normal(k, shape, jnp.float32) * s
    gain = lambda k, shape: 1.0 + 0.02 * jax.random.normal(k, shape, jnp.float32)
    L = DEPTH
    return {
        'x': nrm(ks[0], (BATCH, SEQ, D_MODEL), 1.0),
        'c': nrm(ks[1], (BATCH, D_MODEL), 1.0),
        'w_ada': nrm(ks[2], (L, D_MODEL, 6 * D_MODEL), D_MODEL ** -0.5),
        'b_ada': nrm(ks[3], (L, 6 * D_MODEL), 0.01),
        'norm_mix_pre': gain(ks[4], (L, D_MODEL)),
        'norm_mix_post': gain(ks[5], (L, D_MODEL)),
        'w_in': nrm(ks[6], (L, D_MODEL, IN_COLS), D_MODEL ** -0.5),
        'q_norm': gain(ks[7], (L, HEAD_DIM)),
        'k_norm': gain(ks[8], (L, HEAD_DIM)),
        'hg_lower_bound': nrm(ks[9], (DEPTH + 1, 2, HG_FORGET), 0.5),
        'hg_out_norm': gain(ks[10], (L, HG_HEADS, HG_HEAD_V)),
        'w_out': nrm(ks[11], (L, MIX_WIDTH, D_MODEL), MIX_WIDTH ** -0.5),
        'norm_ffn_pre': gain(ks[12], (L, D_MODEL)),
        'norm_ffn_post': gain(ks[13], (L, D_MODEL)),
        'w_up': nrm(ks[14], (L, D_MODEL, 2 * D_FF), D_MODEL ** -0.5),
        'conv_w': nrm(ks[15], (L, CONV_W, 2 * D_FF), CONV_W ** -0.5),
        'conv_b': nrm(ks[16], (L, 2 * D_FF), 0.01),
        'w_down': nrm(ks[17], (L, D_FF, D_MODEL), D_FF ** -0.5),
    }


def reference(x, c, w_ada, b_ada, norm_mix_pre, norm_mix_post, w_in, q_norm, k_norm,
              hg_lower_bound, hg_out_norm, w_out, norm_ffn_pre, norm_ffn_post,
              w_up, conv_w, conv_b, w_down):
    split_idx = [sum(IN_SIZES[:j]) for j in range(1, len(IN_SIZES))]
    for l in range(DEPTH):
        mod = jax.nn.silu(c) @ w_ada[l] + b_ada[l]
        sh1, sc1, g1, sh2, sc2, g2 = [m[:, None, :] for m in jnp.split(mod, 6, axis=-1)]

        h = rms_norm(x, norm_mix_pre[l]) * (1.0 + sc1) + sh1
        proj = h @ w_in[l]
        aq, ak, av, hq, hf_f, hf_b, hi, hg = jnp.split(proj, split_idx, axis=-1)
        o_att = attention_group(aq, ak, av, q_norm[l], k_norm[l])
        o_hg = hgrn2_group(hq, hf_f, hf_b, hi, hg, hg_lower_bound, hg_out_norm[l], l)
        mix = jnp.concatenate([o_att, o_hg], axis=-1) @ w_out[l]
        x = x + g1 * rms_norm(mix, norm_mix_post[l])

        h = rms_norm(x, norm_ffn_pre[l]) * (1.0 + sc2) + sh2
        y = conv_glu_ffn(h, w_up[l], conv_w[l], conv_b[l], w_down[l])
        x = x + g2 * rms_norm(y, norm_ffn_post[l])
    return x
```

```python
import functools
import math

import jax
import jax.numpy as jnp
from jax import lax
from jax.experimental import pallas as pl
from jax.experimental.pallas import tpu as pltpu

D_MODEL = 2048
GRID_W = 64
ATT_HEADS = 8
ATT_KV_HEADS = 2
ATT_GROUP = ATT_HEADS // ATT_KV_HEADS
HEAD_DIM = 128
ATT_WIDTH = ATT_HEADS * HEAD_DIM
KV_WIDTH = ATT_KV_HEADS * HEAD_DIM
ROPE_THETA = 10000.0
HG_WIDTH = 1024
HG_HEADS = 8
HG_HEAD_V = 128
HG_EXPAND = 128
IN_COLS = ATT_WIDTH + 2 * KV_WIDTH + 5 * HG_WIDTH
D_FF = 5632
EPS = 1e-6

LANES = 128
BF16_SUBLANES = 16
VMEM_LIMIT = 56 * 1024 * 1024

COL_TILE = 512
N_COL_TILES = IN_COLS // COL_TILE
HG_CHUNK = 64
HG_SUB = 16
HG_HALF = 512


def _params(*sem):
    return pltpu.CompilerParams(dimension_semantics=sem, vmem_limit_bytes=VMEM_LIMIT)


def _rms(x, w):
    return x * lax.rsqrt(jnp.mean(x * x, axis=-1, keepdims=True) + EPS) * w


def _sigmoid(x):
    return 1.0 / (1.0 + jnp.exp(-x))


def _silu(x):
    return x * _sigmoid(x)


def _mod_kernel(c_ref, w_ref, b_ref, o_ref):
    s = _silu(c_ref[...])
    o_ref[...] = jnp.sum(s * w_ref[...], axis=0, keepdims=True) + b_ref[...]


def _modulation(c_col, w_ada, b_ada):
    d, n = w_ada.shape
    tn = 1024
    return pl.pallas_call(
        _mod_kernel,
        grid=(n // tn,),
        in_specs=[pl.BlockSpec((d, 1), lambda j: (0, 0)),
                  pl.BlockSpec((d, tn), lambda j: (0, j)),
                  pl.BlockSpec((1, tn), lambda j: (0, j))],
        out_specs=pl.BlockSpec((1, tn), lambda j: (0, j)),
        out_shape=jax.ShapeDtypeStruct((1, n), jnp.float32),
        compiler_params=_params("arbitrary"),
        name="adaln_mod",
    )(c_col, w_ada, b_ada)


def _in_proj_kernel(x_ref, nw_ref, sc_ref, sh_ref, w_ref, o_ref, h_scr):
    @pl.when(pl.program_id(1) == 0)
    def _():
        h = _rms(x_ref[...], nw_ref[...]) * (1.0 + sc_ref[...]) + sh_ref[...]
        h_scr[...] = h.astype(jnp.bfloat16)

    o_ref[0] = jnp.dot(h_scr[...], w_ref[...], preferred_element_type=jnp.float32)


def _in_proj(x2, nw, sc, sh, w_in_bf16, tm):
    s, d = x2.shape
    vec = pl.BlockSpec((1, d), lambda i, j: (0, 0))
    return pl.pallas_call(
        _in_proj_kernel,
        grid=(s // tm, N_COL_TILES),
        in_specs=[pl.BlockSpec((tm, d), lambda i, j: (i, 0)), vec, vec, vec,
                  pl.BlockSpec((d, COL_TILE), lambda i, j: (0, j))],
        out_specs=pl.BlockSpec((1, tm, COL_TILE), lambda i, j: (j, i, 0)),
        out_shape=jax.ShapeDtypeStruct((N_COL_TILES, s, COL_TILE), jnp.float32),
        scratch_shapes=[pltpu.VMEM((tm, d), jnp.bfloat16)],
        compiler_params=_params("parallel", "arbitrary"),
        name="in_proj",
    )(x2, nw, sc, sh, w_in_bf16)


def _rope(y, cos, sin_a, sin_b):
    return (y * cos + pltpu.roll(y, LANES - 32, axis=1) * sin_a
            + pltpu.roll(y, 32, axis=1) * sin_b)


def _att_prep_kernel(q0_ref, q1_ref, kv_ref, cos_ref, sa_ref, sb_ref, qn_ref, kn_ref,
                     q_out, k_out, v_out):
    cos, sa, sb = cos_ref[...], sa_ref[...], sb_ref[...]
    scale = HEAD_DIM ** -0.5
    for t, src in enumerate((q0_ref, q1_ref)):
        for hh in range(COL_TILE // HEAD_DIM):
            xh = src[0, :, hh * HEAD_DIM:(hh + 1) * HEAD_DIM]
            y = _rope(_rms(xh, qn_ref[...]), cos, sa, sb) * scale
            col = t * COL_TILE + hh * HEAD_DIM
            q_out[:, col:col + HEAD_DIM] = y.astype(jnp.bfloat16)
    for hh in range(ATT_KV_HEADS):
        xh = kv_ref[0, :, hh * HEAD_DIM:(hh + 1) * HEAD_DIM]
        y = _rope(_rms(xh, kn_ref[...]), cos, sa, sb)
        k_out[:, hh * HEAD_DIM:(hh + 1) * HEAD_DIM] = y.astype(jnp.bfloat16)
    v_out[...] = kv_ref[0, :, KV_WIDTH:2 * KV_WIDTH].astype(jnp.bfloat16)


def _att_prep(proj, cos, sin_a, sin_b, qn, kn, tm):
    s = proj.shape[1]
    tile = lambda t: pl.BlockSpec((1, tm, COL_TILE), lambda i, t=t: (t, i, 0))
    tab = pl.BlockSpec((tm, HEAD_DIM), lambda i: (i, 0))
    vec = pl.BlockSpec((1, HEAD_DIM), lambda i: (0, 0))
    return pl.pallas_call(
        _att_prep_kernel,
        grid=(s // tm,),
        in_specs=[tile(0), tile(1), tile(2), tab, tab, tab, vec, vec],
        out_specs=[pl.BlockSpec((tm, ATT_WIDTH), lambda i: (i, 0)),
                   pl.BlockSpec((tm, KV_WIDTH), lambda i: (i, 0)),
                   pl.BlockSpec((tm, KV_WIDTH), lambda i: (i, 0))],
        out_shape=[jax.ShapeDtypeStruct((s, ATT_WIDTH), jnp.bfloat16),
                   jax.ShapeDtypeStruct((s, KV_WIDTH), jnp.bfloat16),
                   jax.ShapeDtypeStruct((s, KV_WIDTH), jnp.bfloat16)],
        compiler_params=_params("parallel"),
        name="att_prep",
    )(proj, proj, proj, cos, sin_a, sin_b, qn, kn)


def _attention_kernel(q_ref, k_ref, v_ref, o_ref, m_scr, l_scr, acc_scr, *, tq, tk):
    s_len = k_ref.shape[0]
    q = jnp.concatenate([q_ref[:, g * HEAD_DIM:(g + 1) * HEAD_DIM] for g in range(ATT_GROUP)],
                        axis=0)
    m_scr[...] = jnp.full(m_scr.shape, -jnp.inf, jnp.float32)
    l_scr[...] = jnp.zeros(l_scr.shape, jnp.float32)
    acc_scr[...] = jnp.zeros(acc_scr.shape, jnp.float32)

    def body(j, carry):
        off = pl.multiple_of(j * tk, tk)
        k = k_ref[pl.ds(off, tk), :]
        v = v_ref[pl.ds(off, tk), :]
        s = lax.dot_general(q, k, (((1,), (1,)), ((), ())), preferred_element_type=jnp.float32)
        m_old = m_scr[...]
        m_new = jnp.maximum(m_old, jnp.max(s, axis=-1, keepdims=True))
        alpha = jnp.exp(m_old - m_new)
        p = jnp.exp(s - m_new)
        l_scr[...] = alpha * l_scr[...] + jnp.sum(p, axis=-1, keepdims=True)
        acc_scr[...] = alpha * acc_scr[...] + jnp.dot(p.astype(jnp.bfloat16), v,
                                                      preferred_element_type=jnp.float32)
        m_scr[...] = m_new
        return carry

    lax.fori_loop(0, s_len // tk, body, 0)
    out = acc_scr[...] / l_scr[...]
    for g in range(ATT_GROUP):
        o_ref[:, g * HEAD_DIM:(g + 1) * HEAD_DIM] = out[g * tq:(g + 1) * tq].astype(o_ref.dtype)


def _attention(qt, kt, vt, tq, tk):
    s = qt.shape[0]
    gw = ATT_GROUP * HEAD_DIM
    return pl.pallas_call(
        functools.partial(_attention_kernel, tq=tq, tk=tk),
        grid=(ATT_KV_HEADS, s // tq),
        in_specs=[pl.BlockSpec((tq, gw), lambda h, i: (i, h)),
                  pl.BlockSpec((s, HEAD_DIM), lambda h, i: (0, h)),
                  pl.BlockSpec((s, HEAD_DIM), lambda h, i: (0, h))],
        out_specs=pl.BlockSpec((tq, gw), lambda h, i: (i, h)),
        out_shape=jax.ShapeDtypeStruct((s, ATT_WIDTH), jnp.bfloat16),
        scratch_shapes=[pltpu.VMEM((ATT_GROUP * tq, 1), jnp.float32),
                        pltpu.VMEM((ATT_GROUP * tq, 1), jnp.float32),
                        pltpu.VMEM((ATT_GROUP * tq, HEAD_DIM), jnp.float32)],
        compiler_params=_params("parallel", "arbitrary"),
        name="attention",
    )(qt, kt, vt)


def _hgrn_chunk(qf, kk, g, v, state_ref, reverse):
    c = HG_CHUNK
    nsub = c // HG_SUB
    row = lax.broadcasted_iota(jnp.int32, (c, c), 0)
    col = lax.broadcasted_iota(jnp.int32, (c, c), 1)
    tri = (col >= row) if reverse else (col <= row)
    tri = tri.astype(jnp.bfloat16)
    g1 = g.astype(jnp.bfloat16)
    r1 = g - g1.astype(jnp.float32)
    g2 = r1.astype(jnp.bfloat16)
    g3 = (r1 - g2.astype(jnp.float32)).astype(jnp.bfloat16)
    b = (jnp.dot(tri, g1, preferred_element_type=jnp.float32)
         + jnp.dot(tri, g2, preferred_element_type=jnp.float32)
         + jnp.dot(tri, g3, preferred_element_type=jnp.float32))
    edge = b[0:1, :] if reverse else b[c - 1:c, :]

    sub_pos = lax.broadcasted_iota(jnp.int32, (c, 1), 0) % HG_SUB
    q_in = qf * jnp.exp(b)
    k_out = kk * jnp.exp(edge - b)

    o = None
    for d in range(HG_SUB):
        if d == 0:
            p = qf * kk
            vs = v
        else:
            sh = (c - d) if reverse else d
            ks = pltpu.roll(kk, sh, axis=0)
            bs = pltpu.roll(b, sh, axis=0)
            vs = pltpu.roll(v, sh, axis=0)
            valid = (sub_pos + d < HG_SUB) if reverse else (sub_pos >= d)
            p = jnp.where(valid, qf * ks * jnp.exp(b - bs), 0.0)
        terms = []
        for h in range(HG_HALF // HG_EXPAND):
            sl = slice(h * HG_EXPAND, (h + 1) * HG_EXPAND)
            a = jnp.sum(p[:, sl], axis=-1, keepdims=True)
            terms.append(a * vs[:, sl])
        t = jnp.concatenate(terms, axis=-1)
        o = t if o is None else o + t

    outs = []
    for h in range(HG_HALF // HG_EXPAND):
        sl = slice(h * HG_EXPAND, (h + 1) * HG_EXPAND)
        st = state_ref[h]
        vh = v[:, sl].astype(jnp.bfloat16)
        oh = o[:, sl] + lax.dot_general(q_in[:, sl].astype(jnp.bfloat16), st.astype(jnp.bfloat16),
                                        (((1,), (1,)), ((), ())),
                                        preferred_element_type=jnp.float32)
        pieces = []
        for i in range(nsub):
            rows = slice(i * HG_SUB, (i + 1) * HG_SUB)
            if reverse:
                keys = slice((i + 1) * HG_SUB, c)
                ref_row = (i + 1) * HG_SUB
                has = i < nsub - 1
            else:
                keys = slice(0, i * HG_SUB)
                ref_row = i * HG_SUB - 1
                has = i > 0
            if not has:
                pieces.append(oh[rows])
                continue
            bref = b[ref_row:ref_row + 1, sl]
            qi = (qf[rows, sl] * jnp.exp(b[rows, sl] - bref)).astype(jnp.bfloat16)
            kj = (kk[keys, sl] * jnp.exp(bref - b[keys, sl])).astype(jnp.bfloat16)
            a = lax.dot_general(qi, kj, (((1,), (1,)), ((), ())),
                                preferred_element_type=jnp.float32)
            pieces.append(oh[rows] + jnp.dot(a.astype(jnp.bfloat16), vh[keys],
                                             preferred_element_type=jnp.float32))
        outs.append(jnp.concatenate(pieces, axis=0))
        upd = lax.dot_general(vh, k_out[:, sl].astype(jnp.bfloat16), (((0,), (0,)), ((), ())),
                              preferred_element_type=jnp.float32)
        state_ref[h] = st * jnp.exp(edge[:, sl]) + upd
    return jnp.concatenate(outs, axis=-1)


def _hgrn_kernel(q_ref, f_ref, i_ref, lb_ref, o_ref, state_scr, *, reverse, rows):
    @pl.when(pl.program_id(1) == 0)
    def _():
        state_scr[...] = jnp.zeros(state_scr.shape, jnp.float32)

    lbp = lb_ref[...]
    e = jnp.exp(lbp - jnp.max(lbp, axis=0, keepdims=True))
    lb = e[0:1, :] / jnp.sum(e, axis=0, keepdims=True)
    n_chunks = rows // HG_CHUNK
    order = range(n_chunks - 1, -1, -1) if reverse else range(n_chunks)
    for ci in order:
        r = slice(ci * HG_CHUNK, (ci + 1) * HG_CHUNK)
        qf = _silu(q_ref[0, r, :]) * (HG_EXPAND ** -0.5)
        f = lb + (1.0 - lb) * _sigmoid(f_ref[0, r, :])
        o_ref[r, :] = _hgrn_chunk(qf, 1.0 - f, jnp.log(f), i_ref[0, r, :], state_scr, reverse)


def _hgrn_direction(proj, lb_dir, reverse, rows):
    s = proj.shape[1]
    nblk = s // rows
    n_half = HG_WIDTH // HG_HALF
    blk = (lambda n: nblk - 1 - n) if reverse else (lambda n: n)
    f_tile = 7 if reverse else 5

    def tile(t0):
        return pl.BlockSpec((1, rows, COL_TILE), lambda h, n: (t0 + h, blk(n), 0))

    return pl.pallas_call(
        functools.partial(_hgrn_kernel, reverse=reverse, rows=rows),
        grid=(n_half, nblk),
        in_specs=[tile(3), tile(f_tile), tile(9),
                  pl.BlockSpec((lb_dir.shape[0], HG_HALF), lambda h, n: (0, h))],
        out_specs=pl.BlockSpec((rows, HG_HALF), lambda h, n: (blk(n), h)),
        out_shape=jax.ShapeDtypeStruct((s, HG_WIDTH), jnp.float32),
        scratch_shapes=[pltpu.VMEM((HG_HALF // HG_EXPAND, HG_HEAD_V, HG_EXPAND), jnp.float32)],
        compiler_params=_params("parallel", "arbitrary"),
        name="hgrn_bwd" if reverse else "hgrn_fwd",
    )(proj, proj, proj, lb_dir)


def _mix_out_kernel(att_ref, of_ref, ob_ref, g0_ref, g1_ref, hn_ref, w_ref, x_ref, gate_ref,
                    npost_ref, npre_ref, sc_ref, sh_ref, x1_ref, h2_ref):
    hsum = of_ref[...] + ob_ref[...]
    gate = jnp.concatenate([g0_ref[0], g1_ref[0]], axis=-1)
    heads = []
    for h in range(HG_HEADS):
        sl = slice(h * HG_HEAD_V, (h + 1) * HG_HEAD_V)
        heads.append(_rms(hsum[:, sl], hn_ref[:, sl]) * _silu(gate[:, sl]))
    hg = jnp.concatenate(heads, axis=-1).astype(jnp.bfloat16)
    mix = (jnp.dot(att_ref[...], w_ref[0:ATT_WIDTH, :], preferred_element_type=jnp.float32)
           + jnp.dot(hg, w_ref[ATT_WIDTH:, :], preferred_element_type=jnp.float32))
    x1 = x_ref[...] + gate_ref[...] * _rms(mix, npost_ref[...])
    x1_ref[...] = x1
    h2 = _rms(x1, npre_ref[...]) * (1.0 + sc_ref[...]) + sh_ref[...]
    h2_ref[...] = h2.astype(jnp.bfloat16)


def _mix_out(o_att, o_f, o_b, proj, hn, w_out_bf16, x2, g1, npost, npre, sc2, sh2, tm):
    s, d = x2.shape
    vec = lambda n: pl.BlockSpec((1, n), lambda i: (0, 0))
    row = lambda n: pl.BlockSpec((tm, n), lambda i: (i, 0))
    gtile = lambda t: pl.BlockSpec((1, tm, COL_TILE), lambda i, t=t: (t, i, 0))
    return pl.pallas_call(
        _mix_out_kernel,
        grid=(s // tm,),
        in_specs=[row(ATT_WIDTH), row(HG_WIDTH), row(HG_WIDTH), gtile(11), gtile(12),
                  vec(HG_WIDTH), pl.BlockSpec((d, d), lambda i: (0, 0)), row(d),
                  vec(d), vec(d), vec(d), vec(d), vec(d)],
        out_specs=[row(d), row(d)],
        out_shape=[jax.ShapeDtypeStruct((s, d), jnp.float32),
                   jax.ShapeDtypeStruct((s, d), jnp.bfloat16)],
        compiler_params=_params("parallel"),
        name="mix_out",
    )(o_att, o_f, o_b, proj, proj, hn, w_out_bf16, x2, g1, npost, npre, sc2, sh2)


def _gelu_tanh(x):
    return 0.5 * x * (1.0 + jnp.tanh(math.sqrt(2.0 / math.pi) * (x + 0.044715 * (x * x * x))))


def _ffn_kernel(h_ref, hp_ref, hn_ref, wa_ref, wb_ref, cwa_ref, cwb_ref, cba_ref, cbb_ref,
                wd_ref, x1_ref, gate_ref, npost_ref, o_ref, acc_scr, *, tm):
    i, j = pl.program_id(0), pl.program_id(1)
    first = (i == 0)
    last = (i == pl.num_programs(0) - 1)
    h = h_ref[...]
    row = lax.broadcasted_iota(jnp.int32, (tm, 1), 0)

    def conv_branch(w_ref, cw_ref, cb_ref):
        w = w_ref[...]
        u = jnp.dot(h, w, preferred_element_type=jnp.float32)
        up = jnp.dot(hp_ref[...], w, preferred_element_type=jnp.float32)[BF16_SUBLANES - 1:, :]
        un = jnp.dot(hn_ref[...], w, preferred_element_type=jnp.float32)[0:1, :]
        up = jnp.where(first, 0.0, up)
        un = jnp.where(last, 0.0, un)
        u_prev = jnp.where(row == 0, up, pltpu.roll(u, 1, axis=0))
        u_next = jnp.where(row == tm - 1, un, pltpu.roll(u, tm - 1, axis=0))
        cw = cw_ref[...]
        return cw[0:1, :] * u_prev + cw[1:2, :] * u + cw[2:3, :] * u_next + cb_ref[...]

    a = conv_branch(wa_ref, cwa_ref, cba_ref)
    b = conv_branch(wb_ref, cwb_ref, cbb_ref)
    act = (_gelu_tanh(a) * b).astype(jnp.bfloat16)
    contrib = jnp.dot(act, wd_ref[...], preferred_element_type=jnp.float32)

    @pl.when(j == 0)
    def _():
        acc_scr[...] = contrib

    @pl.when(j > 0)
    def _():
        acc_scr[...] += contrib

    @pl.when(j == pl.num_programs(1) - 1)
    def _():
        o_ref[...] = x1_ref[...] + gate_ref[...] * _rms(acc_scr[...], npost_ref[...])


def _ffn(h2, w_up_bf16, conv_w, conv_b, w_down_bf16, x1, g2, npost, tm, tf):
    s, d = x1.shape
    nf = D_FF // tf
    hb = tm // BF16_SUBLANES
    n_hblk = s // BF16_SUBLANES
    vec = pl.BlockSpec((1, d), lambda i, j: (0, 0))
    return pl.pallas_call(
        functools.partial(_ffn_kernel, tm=tm),
        grid=(s // tm, nf),
        in_specs=[pl.BlockSpec((tm, d), lambda i, j: (i, 0)),
                  pl.BlockSpec((BF16_SUBLANES, d), lambda i, j: (jnp.maximum(i * hb - 1, 0), 0)),
                  pl.BlockSpec((BF16_SUBLANES, d),
                               lambda i, j: (jnp.minimum((i + 1) * hb, n_hblk - 1), 0)),
                  pl.BlockSpec((d, tf), lambda i, j: (0, j)),
                  pl.BlockSpec((d, tf), lambda i, j: (0, j + nf)),
                  pl.BlockSpec((3, tf), lambda i, j: (0, j)),
                  pl.BlockSpec((3, tf), lambda i, j: (0, j + nf)),
                  pl.BlockSpec((1, tf), lambda i, j: (0, j)),
                  pl.BlockSpec((1, tf), lambda i, j: (0, j + nf)),
                  pl.BlockSpec((tf, d), lambda i, j: (j, 0)),
                  pl.BlockSpec((tm, d), lambda i, j: (i, 0)), vec, vec],
        out_specs=pl.BlockSpec((tm, d), lambda i, j: (i, 0)),
        out_shape=jax.ShapeDtypeStruct((s, d), jnp.float32),
        scratch_shapes=[pltpu.VMEM((tm, d), jnp.float32)],
        compiler_params=_params("parallel", "arbitrary"),
        name="conv_ffn",
    )(h2, h2, h2, w_up_bf16, w_up_bf16, conv_w, conv_w, conv_b, conv_b, w_down_bf16, x1, g2, npost)


def _rope_tables(s):
    rows = s // GRID_W
    t = jnp.arange(s)
    r = (t // GRID_W - rows // 2).astype(jnp.float32)
    cpos = (t % GRID_W - GRID_W // 2).astype(jnp.float32)
    axis_dim = HEAD_DIM // 2
    inv = ROPE_THETA ** (-(2.0 * jnp.arange(axis_dim // 2, dtype=jnp.float32)) / axis_dim)
    ang_r = r[:, None] * inv[None, :]
    ang_c = cpos[:, None] * inv[None, :]
    ang = jnp.concatenate([ang_r, ang_r, ang_c, ang_c], axis=-1)
    cos, sin = jnp.cos(ang), jnp.sin(ang)
    low = (jnp.arange(HEAD_DIM) % axis_dim) < (axis_dim // 2)
    return cos, jnp.where(low, -sin, 0.0), jnp.where(low, 0.0, sin)


def kernel(x, c, w_ada, b_ada, norm_mix_pre, norm_mix_post, w_in, q_norm, k_norm, hg_lower_bound,
           hg_out_norm, w_out, norm_ffn_pre, norm_ffn_post, w_up, conv_w, conv_b, w_down):
    batch, s, d = x.shape
    assert batch == 1 and d == D_MODEL and s % GRID_W == 0
    layer = 0
    x2 = x.reshape(s, d)
    tm_big = min(1024, s)
    tm_mid = min(512, s)
    tm_small = min(256, s)

    mod = _modulation(c.reshape(d, 1), w_ada[layer], b_ada[layer].reshape(1, -1))
    sh1, sc1, g1, sh2, sc2, g2 = [mod[:, k * d:(k + 1) * d] for k in range(6)]

    proj = _in_proj(x2, norm_mix_pre[layer].reshape(1, d), sc1, sh1,
                    w_in[layer].astype(jnp.bfloat16), tm_big)

    cos, sin_a, sin_b = _rope_tables(s)
    qt, kt, vt = _att_prep(proj, cos, sin_a, sin_b, q_norm[layer].reshape(1, -1),
                           k_norm[layer].reshape(1, -1), tm_mid)
    o_att = _attention(qt, kt, vt, tq=min(256, s), tk=min(512, s))

    rows = min(256, s)
    o_f = _hgrn_direction(proj, hg_lower_bound[:, 0, :], False, rows)
    o_b = _hgrn_direction(proj, hg_lower_bound[:, 1, :], True, rows)

    x1, h2 = _mix_out(o_att, o_f, o_b, proj, hg_out_norm[layer].reshape(1, -1),
                      w_out[layer].astype(jnp.bfloat16), x2, g1,
                      norm_mix_post[layer].reshape(1, d), norm_ffn_pre[layer].reshape(1, d),
                      sc2, sh2, tm_small)

    out = _ffn(h2, w_up[layer].astype(jnp.bfloat16), conv_w[layer], conv_b[layer].reshape(1, -1),
               w_down[layer].astype(jnp.bfloat16), x1, g2, norm_ffn_post[layer].reshape(1, d),
               tm_mid, 512)
    return out.reshape(batch, s, d)
```

```python
import functools
import math

import jax
import jax.numpy as jnp
from jax import lax
from jax.experimental import pallas as pl
from jax.experimental.pallas import tpu as pltpu

D_MODEL = 2048
GRID_W = 64
ATT_HEADS = 8
ATT_KV_HEADS = 2
ATT_GROUP = ATT_HEADS // ATT_KV_HEADS
HEAD_DIM = 128
ATT_WIDTH = ATT_HEADS * HEAD_DIM
KV_WIDTH = ATT_KV_HEADS * HEAD_DIM
ROPE_THETA = 10000.0
HG_WIDTH = 1024
HG_HEADS = 8
HG_HEAD_V = 128
HG_EXPAND = 128
IN_COLS = ATT_WIDTH + 2 * KV_WIDTH + 5 * HG_WIDTH
D_FF = 5632
EPS = 1e-6
LOG2E = math.log2(math.e)

LANES = 128
BF16_SUBLANES = 16
VMEM_LIMIT = 56 * 1024 * 1024

COL_TILE = 512
N_COL_TILES = IN_COLS // COL_TILE
HG_CHUNK = 64
HG_SUB = 16
HG_HALF = 512
ATT_ROW_BLOCK = 128


def _params(*sem):
    return pltpu.CompilerParams(dimension_semantics=sem, vmem_limit_bytes=VMEM_LIMIT)


def _rms(x, w):
    return x * lax.rsqrt(jnp.mean(x * x, axis=-1, keepdims=True) + EPS) * w


def _sigmoid(x):
    return 1.0 / (1.0 + jnp.exp(-x))


def _silu(x):
    return x * _sigmoid(x)


def _mod_kernel(c_ref, w_ref, b_ref, o_ref):
    s = _silu(c_ref[...])
    o_ref[...] = jnp.sum(s * w_ref[...], axis=0, keepdims=True) + b_ref[...]


def _modulation(c_col, w_ada, b_ada):
    d, n = w_ada.shape
    tn = 1024
    return pl.pallas_call(
        _mod_kernel,
        grid=(n // tn,),
        in_specs=[pl.BlockSpec((d, 1), lambda j: (0, 0)),
                  pl.BlockSpec((d, tn), lambda j: (0, j)),
                  pl.BlockSpec((1, tn), lambda j: (0, j))],
        out_specs=pl.BlockSpec((1, tn), lambda j: (0, j)),
        out_shape=jax.ShapeDtypeStruct((1, n), jnp.float32),
        compiler_params=_params("arbitrary"),
        name="adaln_mod",
    )(c_col, w_ada, b_ada)


def _in_proj_kernel(x_ref, nw_ref, sc_ref, sh_ref, w_ref, o_ref, h_scr):
    @pl.when(pl.program_id(1) == 0)
    def _():
        h = _rms(x_ref[...], nw_ref[...]) * (1.0 + sc_ref[...]) + sh_ref[...]
        h_scr[...] = h.astype(jnp.bfloat16)

    o_ref[0] = jnp.dot(h_scr[...], w_ref[...], preferred_element_type=jnp.float32)


def _in_proj(x2, nw, sc, sh, w_in_bf16, tm):
    s, d = x2.shape
    vec = pl.BlockSpec((1, d), lambda i, j: (0, 0))
    return pl.pallas_call(
        _in_proj_kernel,
        grid=(s // tm, N_COL_TILES),
        in_specs=[pl.BlockSpec((tm, d), lambda i, j: (i, 0)), vec, vec, vec,
                  pl.BlockSpec((d, COL_TILE), lambda i, j: (0, j))],
        out_specs=pl.BlockSpec((1, tm, COL_TILE), lambda i, j: (j, i, 0)),
        out_shape=jax.ShapeDtypeStruct((N_COL_TILES, s, COL_TILE), jnp.float32),
        scratch_shapes=[pltpu.VMEM((tm, d), jnp.bfloat16)],
        compiler_params=_params("parallel", "arbitrary"),
        name="in_proj",
    )(x2, nw, sc, sh, w_in_bf16)


def _rope(y, cos, sin_a, sin_b):
    return (y * cos + pltpu.roll(y, LANES - 32, axis=1) * sin_a
            + pltpu.roll(y, 32, axis=1) * sin_b)


def _att_prep_kernel(q0_ref, q1_ref, kv_ref, cos_ref, sa_ref, sb_ref, qn_ref, kn_ref,
                     q_out, k_out, v_out):
    cos, sa, sb = cos_ref[...], sa_ref[...], sb_ref[...]
    scale = HEAD_DIM ** -0.5 * LOG2E
    for t, src in enumerate((q0_ref, q1_ref)):
        for hh in range(COL_TILE // HEAD_DIM):
            xh = src[0, :, hh * HEAD_DIM:(hh + 1) * HEAD_DIM]
            y = _rope(_rms(xh, qn_ref[...]), cos, sa, sb) * scale
            col = t * COL_TILE + hh * HEAD_DIM
            q_out[:, col:col + HEAD_DIM] = y.astype(jnp.bfloat16)
    for hh in range(ATT_KV_HEADS):
        xh = kv_ref[0, :, hh * HEAD_DIM:(hh + 1) * HEAD_DIM]
        y = _rope(_rms(xh, kn_ref[...]), cos, sa, sb)
        k_out[:, hh * HEAD_DIM:(hh + 1) * HEAD_DIM] = y.astype(jnp.bfloat16)
    ones = jnp.ones((v_out.shape[0], HEAD_DIM), jnp.bfloat16)
    for hh in range(ATT_KV_HEADS):
        vh = kv_ref[0, :, KV_WIDTH + hh * HEAD_DIM:KV_WIDTH + (hh + 1) * HEAD_DIM]
        v_out[:, 2 * hh * HEAD_DIM:(2 * hh + 1) * HEAD_DIM] = vh.astype(jnp.bfloat16)
        v_out[:, (2 * hh + 1) * HEAD_DIM:(2 * hh + 2) * HEAD_DIM] = ones


def _att_prep(proj, cos, sin_a, sin_b, qn, kn, tm):
    s = proj.shape[1]
    tile = lambda t: pl.BlockSpec((1, tm, COL_TILE), lambda i, t=t: (t, i, 0))
    tab = pl.BlockSpec((tm, HEAD_DIM), lambda i: (i, 0))
    vec = pl.BlockSpec((1, HEAD_DIM), lambda i: (0, 0))
    return pl.pallas_call(
        _att_prep_kernel,
        grid=(s // tm,),
        in_specs=[tile(0), tile(1), tile(2), tab, tab, tab, vec, vec],
        out_specs=[pl.BlockSpec((tm, ATT_WIDTH), lambda i: (i, 0)),
                   pl.BlockSpec((tm, KV_WIDTH), lambda i: (i, 0)),
                   pl.BlockSpec((tm, 2 * KV_WIDTH), lambda i: (i, 0))],
        out_shape=[jax.ShapeDtypeStruct((s, ATT_WIDTH), jnp.bfloat16),
                   jax.ShapeDtypeStruct((s, KV_WIDTH), jnp.bfloat16),
                   jax.ShapeDtypeStruct((s, 2 * KV_WIDTH), jnp.bfloat16)],
        compiler_params=_params("parallel"),
        name="att_prep",
    )(proj, proj, proj, cos, sin_a, sin_b, qn, kn)


def _attention_kernel(q_ref, k_ref, v_ref, o_ref, m_scr, acc_scr, alpha_scr, s_scr, p_scr, *,
                      tq, tk):
    n_kv = k_ref.shape[0] // tk
    m_rows = ATT_GROUP * tq
    q = jnp.concatenate([q_ref[:, g * HEAD_DIM:(g + 1) * HEAD_DIM] for g in range(ATT_GROUP)],
                        axis=0)
    m_scr[...] = jnp.full(m_scr.shape, -jnp.inf, jnp.float32)
    acc_scr[...] = jnp.zeros(acc_scr.shape, jnp.float32)

    def scores(j):
        off = pl.multiple_of(j * tk, tk)
        return lax.dot_general(q, k_ref[pl.ds(off, tk), :], (((1,), (1,)), ((), ())),
                               preferred_element_type=jnp.float32)

    def softmax_pv(slot, j):
        for r0 in range(0, m_rows, ATT_ROW_BLOCK):
            rows = slice(r0, r0 + ATT_ROW_BLOCK)
            s = s_scr[slot, rows, :]
            m_old = m_scr[rows, :]
            m_new = jnp.maximum(m_old, jnp.max(s, axis=-1, keepdims=True))
            alpha_scr[rows, :] = jnp.exp2(m_old - m_new)
            m_scr[rows, :] = m_new
            p_scr[rows, :] = jnp.concatenate(
                [jnp.exp2(s[:, t * LANES:(t + 1) * LANES] - m_new) for t in range(tk // LANES)],
                axis=-1).astype(jnp.bfloat16)
        off = pl.multiple_of(j * tk, tk)
        pv = jnp.dot(p_scr[...], v_ref[pl.ds(off, tk), :], preferred_element_type=jnp.float32)
        alpha = alpha_scr[...]
        acc_scr[...] = jnp.concatenate([alpha, alpha], axis=-1) * acc_scr[...] + pv

    s_scr[0] = scores(0)

    def body(jj, carry):
        j = 2 * jj
        s_scr[1] = scores(j + 1)
        softmax_pv(0, j)
        s_scr[0] = scores(jnp.minimum(j + 2, n_kv - 1))
        softmax_pv(1, j + 1)
        return carry

    lax.fori_loop(0, n_kv // 2, body, 0)
    out = acc_scr[:, 0:HEAD_DIM] / acc_scr[:, HEAD_DIM:2 * HEAD_DIM]
    for g in range(ATT_GROUP):
        o_ref[:, g * HEAD_DIM:(g + 1) * HEAD_DIM] = out[g * tq:(g + 1) * tq].astype(o_ref.dtype)


def _attention(qt, kt, vt, tq, tk):
    s = qt.shape[0]
    gw = ATT_GROUP * HEAD_DIM
    return pl.pallas_call(
        functools.partial(_attention_kernel, tq=tq, tk=tk),
        grid=(ATT_KV_HEADS, s // tq),
        in_specs=[pl.BlockSpec((tq, gw), lambda h, i: (i, h)),
                  pl.BlockSpec((s, HEAD_DIM), lambda h, i: (0, h)),
                  pl.BlockSpec((s, 2 * HEAD_DIM), lambda h, i: (0, h))],
        out_specs=pl.BlockSpec((tq, gw), lambda h, i: (i, h)),
        out_shape=jax.ShapeDtypeStruct((s, ATT_WIDTH), jnp.bfloat16),
        scratch_shapes=[pltpu.VMEM((ATT_GROUP * tq, LANES), jnp.float32),
                        pltpu.VMEM((ATT_GROUP * tq, 2 * HEAD_DIM), jnp.float32),
                        pltpu.VMEM((ATT_GROUP * tq, LANES), jnp.float32),
                        pltpu.VMEM((2, ATT_GROUP * tq, tk), jnp.float32),
                        pltpu.VMEM((ATT_GROUP * tq, tk), jnp.bfloat16)],
        compiler_params=_params("parallel", "arbitrary"),
        name="attention",
    )(qt, kt, vt)


def _hgrn_chunk(qf, kk, g, v, state_ref, reverse):
    c = HG_CHUNK
    nsub = c // HG_SUB
    row = lax.broadcasted_iota(jnp.int32, (c, c), 0)
    col = lax.broadcasted_iota(jnp.int32, (c, c), 1)
    tri = (col >= row) if reverse else (col <= row)
    tri = tri.astype(jnp.bfloat16)
    g1 = g.astype(jnp.bfloat16)
    r1 = g - g1.astype(jnp.float32)
    g2 = r1.astype(jnp.bfloat16)
    g3 = (r1 - g2.astype(jnp.float32)).astype(jnp.bfloat16)
    b = (jnp.dot(tri, g1, preferred_element_type=jnp.float32)
         + jnp.dot(tri, g2, preferred_element_type=jnp.float32)
         + jnp.dot(tri, g3, preferred_element_type=jnp.float32))
    edge = b[0:1, :] if reverse else b[c - 1:c, :]

    sub_pos = lax.broadcasted_iota(jnp.int32, (c, 1), 0) % HG_SUB
    q_in = qf * jnp.exp(b)
    k_out = kk * jnp.exp(edge - b)

    o = None
    for d in range(HG_SUB):
        if d == 0:
            p = qf * kk
            vs = v
        else:
            sh = (c - d) if reverse else d
            ks = pltpu.roll(kk, sh, axis=0)
            bs = pltpu.roll(b, sh, axis=0)
            vs = pltpu.roll(v, sh, axis=0)
            valid = (sub_pos + d < HG_SUB) if reverse else (sub_pos >= d)
            p = jnp.where(valid, qf * ks * jnp.exp(b - bs), 0.0)
        terms = []
        for h in range(HG_HALF // HG_EXPAND):
            sl = slice(h * HG_EXPAND, (h + 1) * HG_EXPAND)
            a = jnp.sum(p[:, sl], axis=-1, keepdims=True)
            terms.append(a * vs[:, sl])
        t = jnp.concatenate(terms, axis=-1)
        o = t if o is None else o + t

    outs = []
    for h in range(HG_HALF // HG_EXPAND):
        sl = slice(h * HG_EXPAND, (h + 1) * HG_EXPAND)
        st = state_ref[h]
        vh = v[:, sl].astype(jnp.bfloat16)
        oh = o[:, sl] + lax.dot_general(q_in[:, sl].astype(jnp.bfloat16), st.astype(jnp.bfloat16),
                                        (((1,), (1,)), ((), ())),
                                        preferred_element_type=jnp.float32)
        pieces = []
        for i in range(nsub):
            rows = slice(i * HG_SUB, (i + 1) * HG_SUB)
            if reverse:
                keys = slice((i + 1) * HG_SUB, c)
                ref_row = (i + 1) * HG_SUB
                has = i < nsub - 1
            else:
                keys = slice(0, i * HG_SUB)
                ref_row = i * HG_SUB - 1
                has = i > 0
            if not has:
                pieces.append(oh[rows])
                continue
            bref = b[ref_row:ref_row + 1, sl]
            qi = (qf[rows, sl] * jnp.exp(b[rows, sl] - bref)).astype(jnp.bfloat16)
            kj = (kk[keys, sl] * jnp.exp(bref - b[keys, sl])).astype(jnp.bfloat16)
            a = lax.dot_general(qi, kj, (((1,), (1,)), ((), ())),
                                preferred_element_type=jnp.float32)
            pieces.append(oh[rows] + jnp.dot(a.astype(jnp.bfloat16), vh[keys],
                                             preferred_element_type=jnp.float32))
        outs.append(jnp.concatenate(pieces, axis=0))
        upd = lax.dot_general(vh, k_out[:, sl].astype(jnp.bfloat16), (((0,), (0,)), ((), ())),
                              preferred_element_type=jnp.float32)
        state_ref[h] = st * jnp.exp(edge[:, sl]) + upd
    return jnp.concatenate(outs, axis=-1)


def _hgrn_kernel(q_ref, f_ref, i_ref, lb_ref, o_ref, state_scr, *, reverse, rows):
    @pl.when(pl.program_id(1) == 0)
    def _():
        state_scr[...] = jnp.zeros(state_scr.shape, jnp.float32)

    lbp = lb_ref[...]
    e = jnp.exp(lbp - jnp.max(lbp, axis=0, keepdims=True))
    lb = e[0:1, :] / jnp.sum(e, axis=0, keepdims=True)
    n_chunks = rows // HG_CHUNK
    order = range(n_chunks - 1, -1, -1) if reverse else range(n_chunks)
    for ci in order:
        r = slice(ci * HG_CHUNK, (ci + 1) * HG_CHUNK)
        qf = _silu(q_ref[0, r, :]) * (HG_EXPAND ** -0.5)
        f = lb + (1.0 - lb) * _sigmoid(f_ref[0, r, :])
        o_ref[r, :] = _hgrn_chunk(qf, 1.0 - f, jnp.log(f), i_ref[0, r, :], state_scr, reverse)


def _hgrn_direction(proj, lb_dir, reverse, rows):
    s = proj.shape[1]
    nblk = s // rows
    n_half = HG_WIDTH // HG_HALF
    blk = (lambda n: nblk - 1 - n) if reverse else (lambda n: n)
    f_tile = 7 if reverse else 5

    def tile(t0):
        return pl.BlockSpec((1, rows, COL_TILE), lambda h, n: (t0 + h, blk(n), 0))

    return pl.pallas_call(
        functools.partial(_hgrn_kernel, reverse=reverse, rows=rows),
        grid=(n_half, nblk),
        in_specs=[tile(3), tile(f_tile), tile(9),
                  pl.BlockSpec((lb_dir.shape[0], HG_HALF), lambda h, n: (0, h))],
        out_specs=pl.BlockSpec((rows, HG_HALF), lambda h, n: (blk(n), h)),
        out_shape=jax.ShapeDtypeStruct((s, HG_WIDTH), jnp.float32),
        scratch_shapes=[pltpu.VMEM((HG_HALF // HG_EXPAND, HG_HEAD_V, HG_EXPAND), jnp.float32)],
        compiler_params=_params("parallel", "arbitrary"),
        name="hgrn_bwd" if reverse else "hgrn_fwd",
    )(proj, proj, proj, lb_dir)


def _mix_out_kernel(att_ref, of_ref, ob_ref, g0_ref, g1_ref, hn_ref, w_ref, x_ref, gate_ref,
                    npost_ref, npre_ref, sc_ref, sh_ref, x1_ref, h2_ref):
    hsum = of_ref[...] + ob_ref[...]
    gate = jnp.concatenate([g0_ref[0], g1_ref[0]], axis=-1)
    heads = []
    for h in range(HG_HEADS):
        sl = slice(h * HG_HEAD_V, (h + 1) * HG_HEAD_V)
        heads.append(_rms(hsum[:, sl], hn_ref[:, sl]) * _silu(gate[:, sl]))
    hg = jnp.concatenate(heads, axis=-1).astype(jnp.bfloat16)
    mix = (jnp.dot(att_ref[...], w_ref[0:ATT_WIDTH, :], preferred_element_type=jnp.float32)
           + jnp.dot(hg, w_ref[ATT_WIDTH:, :], preferred_element_type=jnp.float32))
    x1 = x_ref[...] + gate_ref[...] * _rms(mix, npost_ref[...])
    x1_ref[...] = x1
    h2 = _rms(x1, npre_ref[...]) * (1.0 + sc_ref[...]) + sh_ref[...]
    h2_ref[...] = h2.astype(jnp.bfloat16)


def _mix_out(o_att, o_f, o_b, proj, hn, w_out_bf16, x2, g1, npost, npre, sc2, sh2, tm):
    s, d = x2.shape
    vec = lambda n: pl.BlockSpec((1, n), lambda i: (0, 0))
    row = lambda n: pl.BlockSpec((tm, n), lambda i: (i, 0))
    gtile = lambda t: pl.BlockSpec((1, tm, COL_TILE), lambda i, t=t: (t, i, 0))
    return pl.pallas_call(
        _mix_out_kernel,
        grid=(s // tm,),
        in_specs=[row(ATT_WIDTH), row(HG_WIDTH), row(HG_WIDTH), gtile(11), gtile(12),
                  vec(HG_WIDTH), pl.BlockSpec((d, d), lambda i: (0, 0)), row(d),
                  vec(d), vec(d), vec(d), vec(d), vec(d)],
        out_specs=[row(d), row(d)],
        out_shape=[jax.ShapeDtypeStruct((s, d), jnp.float32),
                   jax.ShapeDtypeStruct((s, d), jnp.bfloat16)],
        compiler_params=_params("parallel"),
        name="mix_out",
    )(o_att, o_f, o_b, proj, proj, hn, w_out_bf16, x2, g1, npost, npre, sc2, sh2)


def _gelu_tanh(x):
    return 0.5 * x * (1.0 + jnp.tanh(math.sqrt(2.0 / math.pi) * (x + 0.044715 * (x * x * x))))


def _ffn_kernel(h_ref, hp_ref, hn_ref, wa_ref, wb_ref, cwa_ref, cwb_ref, cba_ref, cbb_ref,
                wd_ref, x1_ref, gate_ref, npost_ref, o_ref, acc_scr, *, tm):
    i, j = pl.program_id(0), pl.program_id(1)
    first = (i == 0)
    last = (i == pl.num_programs(0) - 1)
    h = h_ref[...]
    row = lax.broadcasted_iota(jnp.int32, (tm, 1), 0)

    def conv_branch(w_ref, cw_ref, cb_ref):
        w = w_ref[...]
        u = jnp.dot(h, w, preferred_element_type=jnp.float32)
        up = jnp.dot(hp_ref[...], w, preferred_element_type=jnp.float32)[BF16_SUBLANES - 1:, :]
        un = jnp.dot(hn_ref[...], w, preferred_element_type=jnp.float32)[0:1, :]
        up = jnp.where(first, 0.0, up)
        un = jnp.where(last, 0.0, un)
        u_prev = jnp.where(row == 0, up, pltpu.roll(u, 1, axis=0))
        u_next = jnp.where(row == tm - 1, un, pltpu.roll(u, tm - 1, axis=0))
        cw = cw_ref[...]
        return cw[0:1, :] * u_prev + cw[1:2, :] * u + cw[2:3, :] * u_next + cb_ref[...]

    a = conv_branch(wa_ref, cwa_ref, cba_ref)
    b = conv_branch(wb_ref, cwb_ref, cbb_ref)
    act = (_gelu_tanh(a) * b).astype(jnp.bfloat16)
    contrib = jnp.dot(act, wd_ref[...], preferred_element_type=jnp.float32)

    @pl.when(j == 0)
    def _():
        acc_scr[...] = contrib

    @pl.when(j > 0)
    def _():
        acc_scr[...] += contrib

    @pl.when(j == pl.num_programs(1) - 1)
    def _():
        o_ref[...] = x1_ref[...] + gate_ref[...] * _rms(acc_scr[...], npost_ref[...])


def _ffn(h2, w_up_bf16, conv_w, conv_b, w_down_bf16, x1, g2, npost, tm, tf):
    s, d = x1.shape
    nf = D_FF // tf
    hb = tm // BF16_SUBLANES
    n_hblk = s // BF16_SUBLANES
    vec = pl.BlockSpec((1, d), lambda i, j: (0, 0))
    return pl.pallas_call(
        functools.partial(_ffn_kernel, tm=tm),
        grid=(s // tm, nf),
        in_specs=[pl.BlockSpec((tm, d), lambda i, j: (i, 0)),
                  pl.BlockSpec((BF16_SUBLANES, d), lambda i, j: (jnp.maximum(i * hb - 1, 0), 0)),
                  pl.BlockSpec((BF16_SUBLANES, d),
                               lambda i, j: (jnp.minimum((i + 1) * hb, n_hblk - 1), 0)),
                  pl.BlockSpec((d, tf), lambda i, j: (0, j)),
                  pl.BlockSpec((d, tf), lambda i, j: (0, j + nf)),
                  pl.BlockSpec((3, tf), lambda i, j: (0, j)),
                  pl.BlockSpec((3, tf), lambda i, j: (0, j + nf)),
                  pl.BlockSpec((1, tf), lambda i, j: (0, j)),
                  pl.BlockSpec((1, tf), lambda i, j: (0, j + nf)),
                  pl.BlockSpec((tf, d), lambda i, j: (j, 0)),
                  pl.BlockSpec((tm, d), lambda i, j: (i, 0)), vec, vec],
        out_specs=pl.BlockSpec((tm, d), lambda i, j: (i, 0)),
        out_shape=jax.ShapeDtypeStruct((s, d), jnp.float32),
        scratch_shapes=[pltpu.VMEM((tm, d), jnp.float32)],
        compiler_params=_params("parallel", "arbitrary"),
        name="conv_ffn",
    )(h2, h2, h2, w_up_bf16, w_up_bf16, conv_w, conv_w, conv_b, conv_b, w_down_bf16, x1, g2, npost)


def _rope_tables(s):
    rows = s // GRID_W
    t = jnp.arange(s)
    r = (t // GRID_W - rows // 2).astype(jnp.float32)
    cpos = (t % GRID_W - GRID_W // 2).astype(jnp.float32)
    axis_dim = HEAD_DIM // 2
    inv = ROPE_THETA ** (-(2.0 * jnp.arange(axis_dim // 2, dtype=jnp.float32)) / axis_dim)
    ang_r = r[:, None] * inv[None, :]
    ang_c = cpos[:, None] * inv[None, :]
    ang = jnp.concatenate([ang_r, ang_r, ang_c, ang_c], axis=-1)
    cos, sin = jnp.cos(ang), jnp.sin(ang)
    low = (jnp.arange(HEAD_DIM) % axis_dim) < (axis_dim // 2)
    return cos, jnp.where(low, -sin, 0.0), jnp.where(low, 0.0, sin)


def kernel(x, c, w_ada, b_ada, norm_mix_pre, norm_mix_post, w_in, q_norm, k_norm, hg_lower_bound,
           hg_out_norm, w_out, norm_ffn_pre, norm_ffn_post, w_up, conv_w, conv_b, w_down):
    batch, s, d = x.shape
    assert batch == 1 and d == D_MODEL and s % GRID_W == 0
    layer = 0
    x2 = x.reshape(s, d)
    tm_big = min(1024, s)
    tm_mid = min(512, s)
    tm_small = min(256, s)

    mod = _modulation(c.reshape(d, 1), w_ada[layer], b_ada[layer].reshape(1, -1))
    sh1, sc1, g1, sh2, sc2, g2 = [mod[:, k * d:(k + 1) * d] for k in range(6)]

    proj = _in_proj(x2, norm_mix_pre[layer].reshape(1, d), sc1, sh1,
                    w_in[layer].astype(jnp.bfloat16), tm_big)

    cos, sin_a, sin_b = _rope_tables(s)
    qt, kt, vt = _att_prep(proj, cos, sin_a, sin_b, q_norm[layer].reshape(1, -1),
                           k_norm[layer].reshape(1, -1), tm_mid)
    o_att = _attention(qt, kt, vt, tq=min(256, s), tk=min(512, s))

    rows = min(256, s)
    o_f = _hgrn_direction(proj, hg_lower_bound[:, 0, :], False, rows)
    o_b = _hgrn_direction(proj, hg_lower_bound[:, 1, :], True, rows)

    x1, h2 = _mix_out(o_att, o_f, o_b, proj, hg_out_norm[layer].reshape(1, -1),
                      w_out[layer].astype(jnp.bfloat16), x2, g1,
                      norm_mix_post[layer].reshape(1, d), norm_ffn_pre[layer].reshape(1, d),
                      sc2, sh2, tm_small)

    out = _ffn(h2, w_up[layer].astype(jnp.bfloat16), conv_w[layer], conv_b[layer].reshape(1, -1),
               w_down[layer].astype(jnp.bfloat16), x1, g2, norm_ffn_post[layer].reshape(1, d),
               tm_mid, 512)
    return out.reshape(batch, s, d)
```

```python
import functools
import math

import jax
import jax.numpy as jnp
from jax import lax
from jax.experimental import pallas as pl
from jax.experimental.pallas import tpu as pltpu

D_MODEL = 2048
GRID_W = 64
ATT_HEADS = 8
ATT_KV_HEADS = 2
ATT_GROUP = ATT_HEADS // ATT_KV_HEADS
HEAD_DIM = 128
ATT_WIDTH = ATT_HEADS * HEAD_DIM
KV_WIDTH = ATT_KV_HEADS * HEAD_DIM
ROPE_THETA = 10000.0
HG_WIDTH = 1024
HG_HEADS = 8
HG_HEAD_V = 128
HG_EXPAND = 128
IN_COLS = ATT_WIDTH + 2 * KV_WIDTH + 5 * HG_WIDTH
D_FF = 5632
EPS = 1e-6
LOG2E = math.log2(math.e)

LANES = 128
BF16_SUBLANES = 16
VMEM_LIMIT = 56 * 1024 * 1024
FFN_VMEM_LIMIT = 60 * 1024 * 1024

COL_TILE = 512
N_COL_TILES = IN_COLS // COL_TILE
HG_CHUNK = 64
HG_SUB = 16
HG_HALF = 512
ATT_ROW_BLOCK = 128
FFN_COL_GROUP = 256


def _params(*sem):
    return pltpu.CompilerParams(dimension_semantics=sem, vmem_limit_bytes=VMEM_LIMIT)


def _rms(x, w):
    return x * lax.rsqrt(jnp.mean(x * x, axis=-1, keepdims=True) + EPS) * w


def _sigmoid(x):
    return 1.0 / (1.0 + jnp.exp(-x))


def _silu(x):
    return x * _sigmoid(x)


def _mod_kernel(c_ref, w_ref, b_ref, o_ref):
    s = _silu(c_ref[...])
    o_ref[...] = jnp.sum(s * w_ref[...], axis=0, keepdims=True) + b_ref[...]


def _modulation(c_col, w_ada, b_ada):
    d, n = w_ada.shape
    tn = 1024
    return pl.pallas_call(
        _mod_kernel,
        grid=(n // tn,),
        in_specs=[pl.BlockSpec((d, 1), lambda j: (0, 0)),
                  pl.BlockSpec((d, tn), lambda j: (0, j)),
                  pl.BlockSpec((1, tn), lambda j: (0, j))],
        out_specs=pl.BlockSpec((1, tn), lambda j: (0, j)),
        out_shape=jax.ShapeDtypeStruct((1, n), jnp.float32),
        compiler_params=_params("arbitrary"),
        name="adaln_mod",
    )(c_col, w_ada, b_ada)


def _in_proj_kernel(x_ref, nw_ref, sc_ref, sh_ref, w_ref, o_ref, h_scr):
    @pl.when(pl.program_id(1) == 0)
    def _():
        h = _rms(x_ref[...], nw_ref[...]) * (1.0 + sc_ref[...]) + sh_ref[...]
        h_scr[...] = h.astype(jnp.bfloat16)

    o_ref[0] = jnp.dot(h_scr[...], w_ref[...], preferred_element_type=jnp.float32)


def _in_proj(x2, nw, sc, sh, w_in_bf16, tm):
    s, d = x2.shape
    vec = pl.BlockSpec((1, d), lambda i, j: (0, 0))
    return pl.pallas_call(
        _in_proj_kernel,
        grid=(s // tm, N_COL_TILES),
        in_specs=[pl.BlockSpec((tm, d), lambda i, j: (i, 0)), vec, vec, vec,
                  pl.BlockSpec((d, COL_TILE), lambda i, j: (0, j))],
        out_specs=pl.BlockSpec((1, tm, COL_TILE), lambda i, j: (j, i, 0)),
        out_shape=jax.ShapeDtypeStruct((N_COL_TILES, s, COL_TILE), jnp.float32),
        scratch_shapes=[pltpu.VMEM((tm, d), jnp.bfloat16)],
        compiler_params=_params("parallel", "arbitrary"),
        name="in_proj",
    )(x2, nw, sc, sh, w_in_bf16)


def _rope(y, cos, sin_a, sin_b):
    return (y * cos + pltpu.roll(y, LANES - 32, axis=1) * sin_a
            + pltpu.roll(y, 32, axis=1) * sin_b)


def _att_prep_kernel(q0_ref, q1_ref, kv_ref, cos_ref, sa_ref, sb_ref, qn_ref, kn_ref,
                     q_out, k_out, v_out):
    cos, sa, sb = cos_ref[...], sa_ref[...], sb_ref[...]
    scale = HEAD_DIM ** -0.5 * LOG2E
    for t, src in enumerate((q0_ref, q1_ref)):
        for hh in range(COL_TILE // HEAD_DIM):
            xh = src[0, :, hh * HEAD_DIM:(hh + 1) * HEAD_DIM]
            y = _rope(_rms(xh, qn_ref[...]), cos, sa, sb) * scale
            col = t * COL_TILE + hh * HEAD_DIM
            q_out[:, col:col + HEAD_DIM] = y.astype(jnp.bfloat16)
    for hh in range(ATT_KV_HEADS):
        xh = kv_ref[0, :, hh * HEAD_DIM:(hh + 1) * HEAD_DIM]
        y = _rope(_rms(xh, kn_ref[...]), cos, sa, sb)
        k_out[:, hh * HEAD_DIM:(hh + 1) * HEAD_DIM] = y.astype(jnp.bfloat16)
    ones = jnp.ones((v_out.shape[0], HEAD_DIM), jnp.bfloat16)
    for hh in range(ATT_KV_HEADS):
        vh = kv_ref[0, :, KV_WIDTH + hh * HEAD_DIM:KV_WIDTH + (hh + 1) * HEAD_DIM]
        v_out[:, 2 * hh * HEAD_DIM:(2 * hh + 1) * HEAD_DIM] = vh.astype(jnp.bfloat16)
        v_out[:, (2 * hh + 1) * HEAD_DIM:(2 * hh + 2) * HEAD_DIM] = ones


def _att_prep(proj, cos, sin_a, sin_b, qn, kn, tm):
    s = proj.shape[1]
    tile = lambda t: pl.BlockSpec((1, tm, COL_TILE), lambda i, t=t: (t, i, 0))
    tab = pl.BlockSpec((tm, HEAD_DIM), lambda i: (i, 0))
    vec = pl.BlockSpec((1, HEAD_DIM), lambda i: (0, 0))
    return pl.pallas_call(
        _att_prep_kernel,
        grid=(s // tm,),
        in_specs=[tile(0), tile(1), tile(2), tab, tab, tab, vec, vec],
        out_specs=[pl.BlockSpec((tm, ATT_WIDTH), lambda i: (i, 0)),
                   pl.BlockSpec((tm, KV_WIDTH), lambda i: (i, 0)),
                   pl.BlockSpec((tm, 2 * KV_WIDTH), lambda i: (i, 0))],
        out_shape=[jax.ShapeDtypeStruct((s, ATT_WIDTH), jnp.bfloat16),
                   jax.ShapeDtypeStruct((s, KV_WIDTH), jnp.bfloat16),
                   jax.ShapeDtypeStruct((s, 2 * KV_WIDTH), jnp.bfloat16)],
        compiler_params=_params("parallel"),
        name="att_prep",
    )(proj, proj, proj, cos, sin_a, sin_b, qn, kn)


def _attention_kernel(q_ref, k_ref, v_ref, o_ref, m_scr, acc_scr, alpha_scr, s_scr, p_scr, *,
                      tq, tk):
    n_kv = k_ref.shape[0] // tk
    m_rows = ATT_GROUP * tq
    q = jnp.concatenate([q_ref[:, g * HEAD_DIM:(g + 1) * HEAD_DIM] for g in range(ATT_GROUP)],
                        axis=0)
    m_scr[...] = jnp.full(m_scr.shape, -jnp.inf, jnp.float32)
    acc_scr[...] = jnp.zeros(acc_scr.shape, jnp.float32)

    def scores(j):
        off = pl.multiple_of(j * tk, tk)
        return lax.dot_general(q, k_ref[pl.ds(off, tk), :], (((1,), (1,)), ((), ())),
                               preferred_element_type=jnp.float32)

    def softmax_pv(slot, j):
        for r0 in range(0, m_rows, ATT_ROW_BLOCK):
            rows = slice(r0, r0 + ATT_ROW_BLOCK)
            s = s_scr[slot, rows, :]
            m_old = m_scr[rows, :]
            m_new = jnp.maximum(m_old, jnp.max(s, axis=-1, keepdims=True))
            alpha_scr[rows, :] = jnp.exp2(m_old - m_new)
            m_scr[rows, :] = m_new
            p_scr[rows, :] = jnp.concatenate(
                [jnp.exp2(s[:, t * LANES:(t + 1) * LANES] - m_new) for t in range(tk // LANES)],
                axis=-1).astype(jnp.bfloat16)
        off = pl.multiple_of(j * tk, tk)
        pv = jnp.dot(p_scr[...], v_ref[pl.ds(off, tk), :], preferred_element_type=jnp.float32)
        alpha = alpha_scr[...]
        acc_scr[...] = jnp.concatenate([alpha, alpha], axis=-1) * acc_scr[...] + pv

    s_scr[0] = scores(0)

    def body(jj, carry):
        j = 2 * jj
        s_scr[1] = scores(j + 1)
        softmax_pv(0, j)
        s_scr[0] = scores(jnp.minimum(j + 2, n_kv - 1))
        softmax_pv(1, j + 1)
        return carry

    lax.fori_loop(0, n_kv // 2, body, 0)
    out = acc_scr[:, 0:HEAD_DIM] / acc_scr[:, HEAD_DIM:2 * HEAD_DIM]
    for g in range(ATT_GROUP):
        o_ref[:, g * HEAD_DIM:(g + 1) * HEAD_DIM] = out[g * tq:(g + 1) * tq].astype(o_ref.dtype)


def _attention(qt, kt, vt, tq, tk):
    s = qt.shape[0]
    gw = ATT_GROUP * HEAD_DIM
    return pl.pallas_call(
        functools.partial(_attention_kernel, tq=tq, tk=tk),
        grid=(ATT_KV_HEADS, s // tq),
        in_specs=[pl.BlockSpec((tq, gw), lambda h, i: (i, h)),
                  pl.BlockSpec((s, HEAD_DIM), lambda h, i: (0, h)),
                  pl.BlockSpec((s, 2 * HEAD_DIM), lambda h, i: (0, h))],
        out_specs=pl.BlockSpec((tq, gw), lambda h, i: (i, h)),
        out_shape=jax.ShapeDtypeStruct((s, ATT_WIDTH), jnp.bfloat16),
        scratch_shapes=[pltpu.VMEM((ATT_GROUP * tq, LANES), jnp.float32),
                        pltpu.VMEM((ATT_GROUP * tq, 2 * HEAD_DIM), jnp.float32),
                        pltpu.VMEM((ATT_GROUP * tq, LANES), jnp.float32),
                        pltpu.VMEM((2, ATT_GROUP * tq, tk), jnp.float32),
                        pltpu.VMEM((ATT_GROUP * tq, tk), jnp.bfloat16)],
        compiler_params=_params("parallel", "arbitrary"),
        name="attention",
    )(qt, kt, vt)


def _hgrn_chunk(qf, kk, g, v, state_ref, reverse):
    c = HG_CHUNK
    nsub = c // HG_SUB
    row = lax.broadcasted_iota(jnp.int32, (c, c), 0)
    col = lax.broadcasted_iota(jnp.int32, (c, c), 1)
    tri = (col >= row) if reverse else (col <= row)
    tri = tri.astype(jnp.bfloat16)
    g1 = g.astype(jnp.bfloat16)
    r1 = g - g1.astype(jnp.float32)
    g2 = r1.astype(jnp.bfloat16)
    g3 = (r1 - g2.astype(jnp.float32)).astype(jnp.bfloat16)
    b = (jnp.dot(tri, g1, preferred_element_type=jnp.float32)
         + jnp.dot(tri, g2, preferred_element_type=jnp.float32)
         + jnp.dot(tri, g3, preferred_element_type=jnp.float32))
    edge = b[0:1, :] if reverse else b[c - 1:c, :]

    sub_pos = lax.broadcasted_iota(jnp.int32, (c, 1), 0) % HG_SUB
    q_in = qf * jnp.exp(b)
    k_out = kk * jnp.exp(edge - b)

    o = None
    for d in range(HG_SUB):
        if d == 0:
            p = qf * kk
            vs = v
        else:
            sh = (c - d) if reverse else d
            ks = pltpu.roll(kk, sh, axis=0)
            bs = pltpu.roll(b, sh, axis=0)
            vs = pltpu.roll(v, sh, axis=0)
            valid = (sub_pos + d < HG_SUB) if reverse else (sub_pos >= d)
            p = jnp.where(valid, qf * ks * jnp.exp(b - bs), 0.0)
        terms = []
        for h in range(HG_HALF // HG_EXPAND):
            sl = slice(h * HG_EXPAND, (h + 1) * HG_EXPAND)
            a = jnp.sum(p[:, sl], axis=-1, keepdims=True)
            terms.append(a * vs[:, sl])
        t = jnp.concatenate(terms, axis=-1)
        o = t if o is None else o + t

    outs = []
    for h in range(HG_HALF // HG_EXPAND):
        sl = slice(h * HG_EXPAND, (h + 1) * HG_EXPAND)
        st = state_ref[h]
        vh = v[:, sl].astype(jnp.bfloat16)
        oh = o[:, sl] + lax.dot_general(q_in[:, sl].astype(jnp.bfloat16), st.astype(jnp.bfloat16),
                                        (((1,), (1,)), ((), ())),
                                        preferred_element_type=jnp.float32)
        pieces = []
        for i in range(nsub):
            rows = slice(i * HG_SUB, (i + 1) * HG_SUB)
            if reverse:
                keys = slice((i + 1) * HG_SUB, c)
                ref_row = (i + 1) * HG_SUB
                has = i < nsub - 1
            else:
                keys = slice(0, i * HG_SUB)
                ref_row = i * HG_SUB - 1
                has = i > 0
            if not has:
                pieces.append(oh[rows])
                continue
            bref = b[ref_row:ref_row + 1, sl]
            qi = (qf[rows, sl] * jnp.exp(b[rows, sl] - bref)).astype(jnp.bfloat16)
            kj = (kk[keys, sl] * jnp.exp(bref - b[keys, sl])).astype(jnp.bfloat16)
            a = lax.dot_general(qi, kj, (((1,), (1,)), ((), ())),
                                preferred_element_type=jnp.float32)
            pieces.append(oh[rows] + jnp.dot(a.astype(jnp.bfloat16), vh[keys],
                                             preferred_element_type=jnp.float32))
        outs.append(jnp.concatenate(pieces, axis=0))
        upd = lax.dot_general(vh, k_out[:, sl].astype(jnp.bfloat16), (((0,), (0,)), ((), ())),
                              preferred_element_type=jnp.float32)
        state_ref[h] = st * jnp.exp(edge[:, sl]) + upd
    return jnp.concatenate(outs, axis=-1)


def _hgrn_kernel(q_ref, f_ref, i_ref, lb_ref, o_ref, state_scr, *, reverse, rows):
    @pl.when(pl.program_id(1) == 0)
    def _():
        state_scr[...] = jnp.zeros(state_scr.shape, jnp.float32)

    lbp = lb_ref[...]
    e = jnp.exp(lbp - jnp.max(lbp, axis=0, keepdims=True))
    lb = e[0:1, :] / jnp.sum(e, axis=0, keepdims=True)
    n_chunks = rows // HG_CHUNK
    order = range(n_chunks - 1, -1, -1) if reverse else range(n_chunks)
    for ci in order:
        r = slice(ci * HG_CHUNK, (ci + 1) * HG_CHUNK)
        qf = _silu(q_ref[0, r, :]) * (HG_EXPAND ** -0.5)
        f = lb + (1.0 - lb) * _sigmoid(f_ref[0, r, :])
        o_ref[r, :] = _hgrn_chunk(qf, 1.0 - f, jnp.log(f), i_ref[0, r, :], state_scr, reverse)


def _hgrn_direction(proj, lb_dir, reverse, rows):
    s = proj.shape[1]
    nblk = s // rows
    n_half = HG_WIDTH // HG_HALF
    blk = (lambda n: nblk - 1 - n) if reverse else (lambda n: n)
    f_tile = 7 if reverse else 5

    def tile(t0):
        return pl.BlockSpec((1, rows, COL_TILE), lambda h, n: (t0 + h, blk(n), 0))

    return pl.pallas_call(
        functools.partial(_hgrn_kernel, reverse=reverse, rows=rows),
        grid=(n_half, nblk),
        in_specs=[tile(3), tile(f_tile), tile(9),
                  pl.BlockSpec((lb_dir.shape[0], HG_HALF), lambda h, n: (0, h))],
        out_specs=pl.BlockSpec((rows, HG_HALF), lambda h, n: (blk(n), h)),
        out_shape=jax.ShapeDtypeStruct((s, HG_WIDTH), jnp.float32),
        scratch_shapes=[pltpu.VMEM((HG_HALF // HG_EXPAND, HG_HEAD_V, HG_EXPAND), jnp.float32)],
        compiler_params=_params("parallel", "arbitrary"),
        name="hgrn_bwd" if reverse else "hgrn_fwd",
    )(proj, proj, proj, lb_dir)


def _mix_out_kernel(att_ref, of_ref, ob_ref, g0_ref, g1_ref, hn_ref, w_ref, x_ref, gate_ref,
                    npost_ref, npre_ref, sc_ref, sh_ref, x1_ref, h2_ref):
    hsum = of_ref[...] + ob_ref[...]
    gate = jnp.concatenate([g0_ref[0], g1_ref[0]], axis=-1)
    heads = []
    for h in range(HG_HEADS):
        sl = slice(h * HG_HEAD_V, (h + 1) * HG_HEAD_V)
        heads.append(_rms(hsum[:, sl], hn_ref[:, sl]) * _silu(gate[:, sl]))
    hg = jnp.concatenate(heads, axis=-1).astype(jnp.bfloat16)
    mix = (jnp.dot(att_ref[...], w_ref[0:ATT_WIDTH, :], preferred_element_type=jnp.float32)
           + jnp.dot(hg, w_ref[ATT_WIDTH:, :], preferred_element_type=jnp.float32))
    x1 = x_ref[...] + gate_ref[...] * _rms(mix, npost_ref[...])
    x1_ref[...] = x1
    h2 = _rms(x1, npre_ref[...]) * (1.0 + sc_ref[...]) + sh_ref[...]
    h2_ref[...] = h2.astype(jnp.bfloat16)


def _mix_out(o_att, o_f, o_b, proj, hn, w_out_bf16, x2, g1, npost, npre, sc2, sh2, tm):
    s, d = x2.shape
    vec = lambda n: pl.BlockSpec((1, n), lambda i: (0, 0))
    row = lambda n: pl.BlockSpec((tm, n), lambda i: (i, 0))
    gtile = lambda t: pl.BlockSpec((1, tm, COL_TILE), lambda i, t=t: (t, i, 0))
    return pl.pallas_call(
        _mix_out_kernel,
        grid=(s // tm,),
        in_specs=[row(ATT_WIDTH), row(HG_WIDTH), row(HG_WIDTH), gtile(11), gtile(12),
                  vec(HG_WIDTH), pl.BlockSpec((d, d), lambda i: (0, 0)), row(d),
                  vec(d), vec(d), vec(d), vec(d), vec(d)],
        out_specs=[row(d), row(d)],
        out_shape=[jax.ShapeDtypeStruct((s, d), jnp.float32),
                   jax.ShapeDtypeStruct((s, d), jnp.bfloat16)],
        compiler_params=_params("parallel"),
        name="mix_out",
    )(o_att, o_f, o_b, proj, proj, hn, w_out_bf16, x2, g1, npost, npre, sc2, sh2)


def _gelu_tanh(x):
    return 0.5 * x * (1.0 + jnp.tanh(math.sqrt(2.0 / math.pi) * (x + 0.044715 * (x * x * x))))


def _ffn_kernel(h_ref, hp_ref, hn_ref, wa_ref, wb_ref, cwa_ref, cwb_ref, cba_ref, cbb_ref,
                wd_ref, x1_ref, gate_ref, npost_ref, o_ref, hext_scr, *, tm, tf):
    i, j = pl.program_id(0), pl.program_id(1)
    halo = BF16_SUBLANES
    n_ext = tm + 2 * halo

    @pl.when(j == 0)
    def _():
        hext_scr[0:halo, :] = jnp.where(i == 0, jnp.zeros_like(hp_ref[...]), hp_ref[...])
        hext_scr[halo:halo + tm, :] = h_ref[...]
        hext_scr[halo + tm:, :] = jnp.where(i == pl.num_programs(0) - 1,
                                            jnp.zeros_like(hn_ref[...]), hn_ref[...])
        o_ref[...] = jnp.zeros(o_ref.shape, o_ref.dtype)

    h_ext = hext_scr[...]

    def conv(u, cw, cb):
        u_prev = pltpu.roll(u, 1, axis=0)[halo:halo + tm]
        u_next = pltpu.roll(u, n_ext - 1, axis=0)[halo:halo + tm]
        return cw[0:1, :] * u_prev + cw[1:2, :] * u[halo:halo + tm] + cw[2:3, :] * u_next + cb

    groups = [slice(c0, c0 + FFN_COL_GROUP) for c0 in range(0, tf, FFN_COL_GROUP)]
    ups = [(jnp.dot(h_ext, wa_ref[:, cols], preferred_element_type=jnp.float32),
            jnp.dot(h_ext, wb_ref[:, cols], preferred_element_type=jnp.float32)) for cols in groups]
    for cols, (ua, ub) in zip(groups, ups):
        a = conv(ua, cwa_ref[:, cols], cba_ref[:, cols])
        b = conv(ub, cwb_ref[:, cols], cbb_ref[:, cols])
        act = (_gelu_tanh(a) * b).astype(jnp.bfloat16)
        o_ref[...] += jnp.dot(act, wd_ref[cols, :], preferred_element_type=jnp.float32)

    @pl.when(j == pl.num_programs(1) - 1)
    def _():
        o_ref[...] = x1_ref[...] + gate_ref[...] * _rms(o_ref[...], npost_ref[...])


def _ffn(h2, w_up_bf16, conv_w, conv_b, w_down_bf16, x1, g2, npost, tm, tf):
    s, d = x1.shape
    nf = D_FF // tf
    hb = tm // BF16_SUBLANES
    n_hblk = s // BF16_SUBLANES
    vec = pl.BlockSpec((1, d), lambda i, j: (0, 0))
    single = pl.Buffered(1)
    return pl.pallas_call(
        functools.partial(_ffn_kernel, tm=tm, tf=tf),
        grid=(s // tm, nf),
        in_specs=[pl.BlockSpec((tm, d), lambda i, j: (i, 0), pipeline_mode=single),
                  pl.BlockSpec((BF16_SUBLANES, d), lambda i, j: (jnp.maximum(i * hb - 1, 0), 0)),
                  pl.BlockSpec((BF16_SUBLANES, d),
                               lambda i, j: (jnp.minimum((i + 1) * hb, n_hblk - 1), 0)),
                  pl.BlockSpec((d, tf), lambda i, j: (0, j)),
                  pl.BlockSpec((d, tf), lambda i, j: (0, j + nf)),
                  pl.BlockSpec((3, tf), lambda i, j: (0, j)),
                  pl.BlockSpec((3, tf), lambda i, j: (0, j + nf)),
                  pl.BlockSpec((1, tf), lambda i, j: (0, j)),
                  pl.BlockSpec((1, tf), lambda i, j: (0, j + nf)),
                  pl.BlockSpec((tf, d), lambda i, j: (j, 0)),
                  pl.BlockSpec((tm, d), lambda i, j: (i, 0), pipeline_mode=single), vec, vec],
        out_specs=pl.BlockSpec((tm, d), lambda i, j: (i, 0)),
        out_shape=jax.ShapeDtypeStruct((s, d), jnp.float32),
        scratch_shapes=[pltpu.VMEM((tm + 2 * BF16_SUBLANES, d), jnp.bfloat16)],
        compiler_params=pltpu.CompilerParams(dimension_semantics=("parallel", "arbitrary"),
                                             vmem_limit_bytes=FFN_VMEM_LIMIT),
        name="conv_ffn",
    )(h2, h2, h2, w_up_bf16, w_up_bf16, conv_w, conv_w, conv_b, conv_b, w_down_bf16, x1, g2, npost)


def _rope_tables(s):
    rows = s // GRID_W
    t = jnp.arange(s)
    r = (t // GRID_W - rows // 2).astype(jnp.float32)
    cpos = (t % GRID_W - GRID_W // 2).astype(jnp.float32)
    axis_dim = HEAD_DIM // 2
    inv = ROPE_THETA ** (-(2.0 * jnp.arange(axis_dim // 2, dtype=jnp.float32)) / axis_dim)
    ang_r = r[:, None] * inv[None, :]
    ang_c = cpos[:, None] * inv[None, :]
    ang = jnp.concatenate([ang_r, ang_r, ang_c, ang_c], axis=-1)
    cos, sin = jnp.cos(ang), jnp.sin(ang)
    low = (jnp.arange(HEAD_DIM) % axis_dim) < (axis_dim // 2)
    return cos, jnp.where(low, -sin, 0.0), jnp.where(low, 0.0, sin)


def kernel(x, c, w_ada, b_ada, norm_mix_pre, norm_mix_post, w_in, q_norm, k_norm, hg_lower_bound,
           hg_out_norm, w_out, norm_ffn_pre, norm_ffn_post, w_up, conv_w, conv_b, w_down):
    batch, s, d = x.shape
    assert batch == 1 and d == D_MODEL and s % GRID_W == 0
    layer = 0
    x2 = x.reshape(s, d)
    tm_big = min(1024, s)
    tm_mid = min(512, s)
    tm_small = min(256, s)

    mod = _modulation(c.reshape(d, 1), w_ada[layer], b_ada[layer].reshape(1, -1))
    sh1, sc1, g1, sh2, sc2, g2 = [mod[:, k * d:(k + 1) * d] for k in range(6)]

    proj = _in_proj(x2, norm_mix_pre[layer].reshape(1, d), sc1, sh1,
                    w_in[layer].astype(jnp.bfloat16), tm_big)

    cos, sin_a, sin_b = _rope_tables(s)
    qt, kt, vt = _att_prep(proj, cos, sin_a, sin_b, q_norm[layer].reshape(1, -1),
                           k_norm[layer].reshape(1, -1), tm_mid)
    o_att = _attention(qt, kt, vt, tq=min(256, s), tk=min(512, s))

    rows = min(256, s)
    o_f = _hgrn_direction(proj, hg_lower_bound[:, 0, :], False, rows)
    o_b = _hgrn_direction(proj, hg_lower_bound[:, 1, :], True, rows)

    x1, h2 = _mix_out(o_att, o_f, o_b, proj, hg_out_norm[layer].reshape(1, -1),
                      w_out[layer].astype(jnp.bfloat16), x2, g1,
                      norm_mix_post[layer].reshape(1, d), norm_ffn_pre[layer].reshape(1, d),
                      sc2, sh2, tm_small)

    out = _ffn(h2, w_up[layer].astype(jnp.bfloat16), conv_w[layer], conv_b[layer].reshape(1, -1),
               w_down[layer].astype(jnp.bfloat16), x1, g2, norm_ffn_post[layer].reshape(1, d),
               tm_big, 512)
    return out.reshape(batch, s, d)
```

```python
import functools
import math

import numpy as np
import jax
import jax.numpy as jnp
from jax import lax
from jax.experimental import pallas as pl
from jax.experimental.pallas import tpu as pltpu

D_MODEL = 2048
GRID_W = 64
ATT_HEADS = 8
ATT_KV_HEADS = 2
ATT_GROUP = ATT_HEADS // ATT_KV_HEADS
HEAD_DIM = 128
ATT_WIDTH = ATT_HEADS * HEAD_DIM
KV_WIDTH = ATT_KV_HEADS * HEAD_DIM
ROPE_THETA = 10000.0
HG_WIDTH = 1024
HG_HEADS = 8
HG_HEAD_V = 128
HG_EXPAND = 128
IN_COLS = ATT_WIDTH + 2 * KV_WIDTH + 5 * HG_WIDTH
D_FF = 5632
EPS = 1e-6
LOG2E = math.log2(math.e)

LANES = 128
BF16_SUBLANES = 16
VMEM_LIMIT = 56 * 1024 * 1024
FFN_VMEM_LIMIT = 60 * 1024 * 1024

COL_TILE = 512
N_COL_TILES = IN_COLS // COL_TILE
HG_CHUNK = 64
HG_LEVELS = (32, 16, 8, 4, 2, 1)
HG_HALF = 512
ATT_ROW_BLOCK = 128
FFN_COL_GROUP = 256


def _params(*sem):
    return pltpu.CompilerParams(dimension_semantics=sem, vmem_limit_bytes=VMEM_LIMIT)


def _rms(x, w):
    return x * lax.rsqrt(jnp.mean(x * x, axis=-1, keepdims=True) + EPS) * w


def _sigmoid(x):
    return 1.0 / (1.0 + jnp.exp(-x))


def _silu(x):
    return x * _sigmoid(x)


def _mod_kernel(c_ref, w_ref, b_ref, o_ref):
    s = _silu(c_ref[...])
    o_ref[...] = jnp.sum(s * w_ref[...], axis=0, keepdims=True) + b_ref[...]


def _modulation(c_col, w_ada, b_ada):
    d, n = w_ada.shape
    tn = 1024
    return pl.pallas_call(
        _mod_kernel,
        grid=(n // tn,),
        in_specs=[pl.BlockSpec((d, 1), lambda j: (0, 0)),
                  pl.BlockSpec((d, tn), lambda j: (0, j)),
                  pl.BlockSpec((1, tn), lambda j: (0, j))],
        out_specs=pl.BlockSpec((1, tn), lambda j: (0, j)),
        out_shape=jax.ShapeDtypeStruct((1, n), jnp.float32),
        compiler_params=_params("arbitrary"),
        name="adaln_mod",
    )(c_col, w_ada, b_ada)


def _in_proj_kernel(x_ref, nw_ref, sc_ref, sh_ref, w_ref, o_ref, h_scr):
    @pl.when(pl.program_id(1) == 0)
    def _():
        h = _rms(x_ref[...], nw_ref[...]) * (1.0 + sc_ref[...]) + sh_ref[...]
        h_scr[...] = h.astype(jnp.bfloat16)

    o_ref[0] = jnp.dot(h_scr[...], w_ref[...], preferred_element_type=jnp.float32)


def _in_proj(x2, nw, sc, sh, w_in_bf16, tm):
    s, d = x2.shape
    vec = pl.BlockSpec((1, d), lambda i, j: (0, 0))
    return pl.pallas_call(
        _in_proj_kernel,
        grid=(s // tm, N_COL_TILES),
        in_specs=[pl.BlockSpec((tm, d), lambda i, j: (i, 0)), vec, vec, vec,
                  pl.BlockSpec((d, COL_TILE), lambda i, j: (0, j))],
        out_specs=pl.BlockSpec((1, tm, COL_TILE), lambda i, j: (j, i, 0)),
        out_shape=jax.ShapeDtypeStruct((N_COL_TILES, s, COL_TILE), jnp.float32),
        scratch_shapes=[pltpu.VMEM((tm, d), jnp.bfloat16)],
        compiler_params=_params("parallel", "arbitrary"),
        name="in_proj",
    )(x2, nw, sc, sh, w_in_bf16)


def _rope(y, cos, sin_a, sin_b):
    return (y * cos + pltpu.roll(y, LANES - 32, axis=1) * sin_a
            + pltpu.roll(y, 32, axis=1) * sin_b)


def _att_prep_kernel(q0_ref, q1_ref, kv_ref, cos_ref, sa_ref, sb_ref, qn_ref, kn_ref,
                     q_out, k_out, v_out):
    cos, sa, sb = cos_ref[...], sa_ref[...], sb_ref[...]
    scale = HEAD_DIM ** -0.5 * LOG2E
    for t, src in enumerate((q0_ref, q1_ref)):
        for hh in range(COL_TILE // HEAD_DIM):
            xh = src[0, :, hh * HEAD_DIM:(hh + 1) * HEAD_DIM]
            y = _rope(_rms(xh, qn_ref[...]), cos, sa, sb) * scale
            col = t * COL_TILE + hh * HEAD_DIM
            q_out[:, col:col + HEAD_DIM] = y.astype(jnp.bfloat16)
    for hh in range(ATT_KV_HEADS):
        xh = kv_ref[0, :, hh * HEAD_DIM:(hh + 1) * HEAD_DIM]
        y = _rope(_rms(xh, kn_ref[...]), cos, sa, sb)
        k_out[:, hh * HEAD_DIM:(hh + 1) * HEAD_DIM] = y.astype(jnp.bfloat16)
    ones = jnp.ones((v_out.shape[0], HEAD_DIM), jnp.bfloat16)
    for hh in range(ATT_KV_HEADS):
        vh = kv_ref[0, :, KV_WIDTH + hh * HEAD_DIM:KV_WIDTH + (hh + 1) * HEAD_DIM]
        v_out[:, 2 * hh * HEAD_DIM:(2 * hh + 1) * HEAD_DIM] = vh.astype(jnp.bfloat16)
        v_out[:, (2 * hh + 1) * HEAD_DIM:(2 * hh + 2) * HEAD_DIM] = ones


def _att_prep(proj, cos, sin_a, sin_b, qn, kn, tm):
    s = proj.shape[1]
    tile = lambda t: pl.BlockSpec((1, tm, COL_TILE), lambda i, t=t: (t, i, 0))
    tab = pl.BlockSpec((tm, HEAD_DIM), lambda i: (i, 0))
    vec = pl.BlockSpec((1, HEAD_DIM), lambda i: (0, 0))
    return pl.pallas_call(
        _att_prep_kernel,
        grid=(s // tm,),
        in_specs=[tile(0), tile(1), tile(2), tab, tab, tab, vec, vec],
        out_specs=[pl.BlockSpec((tm, ATT_WIDTH), lambda i: (i, 0)),
                   pl.BlockSpec((tm, KV_WIDTH), lambda i: (i, 0)),
                   pl.BlockSpec((tm, 2 * KV_WIDTH), lambda i: (i, 0))],
        out_shape=[jax.ShapeDtypeStruct((s, ATT_WIDTH), jnp.bfloat16),
                   jax.ShapeDtypeStruct((s, KV_WIDTH), jnp.bfloat16),
                   jax.ShapeDtypeStruct((s, 2 * KV_WIDTH), jnp.bfloat16)],
        compiler_params=_params("parallel"),
        name="att_prep",
    )(proj, proj, proj, cos, sin_a, sin_b, qn, kn)


def _attention_kernel(q_ref, k_ref, v_ref, o_ref, m_scr, acc_scr, alpha_scr, s_scr, p_scr, *,
                      tq, tk):
    n_kv = k_ref.shape[0] // tk
    m_rows = ATT_GROUP * tq
    q = jnp.concatenate([q_ref[:, g * HEAD_DIM:(g + 1) * HEAD_DIM] for g in range(ATT_GROUP)],
                        axis=0)
    m_scr[...] = jnp.full(m_scr.shape, -jnp.inf, jnp.float32)
    acc_scr[...] = jnp.zeros(acc_scr.shape, jnp.float32)

    def scores(j):
        off = pl.multiple_of(j * tk, tk)
        return lax.dot_general(q, k_ref[pl.ds(off, tk), :], (((1,), (1,)), ((), ())),
                               preferred_element_type=jnp.float32)

    def softmax_pv(slot, j):
        for r0 in range(0, m_rows, ATT_ROW_BLOCK):
            rows = slice(r0, r0 + ATT_ROW_BLOCK)
            s = s_scr[slot, rows, :]
            m_old = m_scr[rows, :]
            m_new = jnp.maximum(m_old, jnp.max(s, axis=-1, keepdims=True))
            alpha_scr[rows, :] = jnp.exp2(m_old - m_new)
            m_scr[rows, :] = m_new
            p_scr[rows, :] = jnp.concatenate(
                [jnp.exp2(s[:, t * LANES:(t + 1) * LANES] - m_new) for t in range(tk // LANES)],
                axis=-1).astype(jnp.bfloat16)
        off = pl.multiple_of(j * tk, tk)
        pv = jnp.dot(p_scr[...], v_ref[pl.ds(off, tk), :], preferred_element_type=jnp.float32)
        alpha = alpha_scr[...]
        acc_scr[...] = jnp.concatenate([alpha, alpha], axis=-1) * acc_scr[...] + pv

    s_scr[0] = scores(0)

    def body(jj, carry):
        j = 2 * jj
        s_scr[1] = scores(j + 1)
        softmax_pv(0, j)
        s_scr[0] = scores(jnp.minimum(j + 2, n_kv - 1))
        softmax_pv(1, j + 1)
        return carry

    lax.fori_loop(0, n_kv // 2, body, 0)
    out = acc_scr[:, 0:HEAD_DIM] / acc_scr[:, HEAD_DIM:2 * HEAD_DIM]
    for g in range(ATT_GROUP):
        o_ref[:, g * HEAD_DIM:(g + 1) * HEAD_DIM] = out[g * tq:(g + 1) * tq].astype(o_ref.dtype)


def _attention(qt, kt, vt, tq, tk):
    s = qt.shape[0]
    gw = ATT_GROUP * HEAD_DIM
    return pl.pallas_call(
        functools.partial(_attention_kernel, tq=tq, tk=tk),
        grid=(ATT_KV_HEADS, s // tq),
        in_specs=[pl.BlockSpec((tq, gw), lambda h, i: (i, h)),
                  pl.BlockSpec((s, HEAD_DIM), lambda h, i: (0, h)),
                  pl.BlockSpec((s, 2 * HEAD_DIM), lambda h, i: (0, h))],
        out_specs=pl.BlockSpec((tq, gw), lambda h, i: (i, h)),
        out_shape=jax.ShapeDtypeStruct((s, ATT_WIDTH), jnp.bfloat16),
        scratch_shapes=[pltpu.VMEM((ATT_GROUP * tq, LANES), jnp.float32),
                        pltpu.VMEM((ATT_GROUP * tq, 2 * HEAD_DIM), jnp.float32),
                        pltpu.VMEM((ATT_GROUP * tq, LANES), jnp.float32),
                        pltpu.VMEM((2, ATT_GROUP * tq, tk), jnp.float32),
                        pltpu.VMEM((ATT_GROUP * tq, tk), jnp.bfloat16)],
        compiler_params=_params("parallel", "arbitrary"),
        name="attention",
    )(qt, kt, vt)


def _hgrn_tables(reverse):
    c = HG_CHUNK
    idx = np.arange(c)
    row, u = idx[:, None], idx[None, :]
    blocks = [(u >= row) if reverse else (u <= row), (u < row) if reverse else (u > row)]
    for h in HG_LEVELS:
        last_first = ((idx // (2 * h)) * 2 * h + h - 1)[:, None]
        if reverse:
            is_q = (idx % (2 * h) < h)[:, None]
            blocks.append(np.where(is_q, (u >= row) & (u <= last_first), (u > last_first) & (u < row)))
        else:
            is_q = (idx % (2 * h) >= h)[:, None]
            blocks.append(np.where(is_q, (u > last_first) & (u <= row), (u > row) & (u <= last_first)))
    w = np.concatenate(blocks, axis=0).astype(np.float32)
    return jnp.asarray(np.tile(w, (1, 3)), jnp.bfloat16)


def _hgrn_level_masks(reverse):
    c = HG_CHUNK
    row = lax.broadcasted_iota(jnp.int32, (c, c), 0)
    col = lax.broadcasted_iota(jnp.int32, (c, c), 1)
    is_query, pair_mask = [], []
    for h in HG_LEVELS:
        q_row = (row % (2 * h) < h) if reverse else (row % (2 * h) >= h)
        q_col = (col % (2 * h) < h) if reverse else (col % (2 * h) >= h)
        same = (row // (2 * h)) == (col // (2 * h))
        pair_mask.append((same & q_row & jnp.logical_not(q_col)).astype(jnp.float32))
        is_query.append(q_row[:, 0:1])
    return is_query, pair_mask


def _hgrn_chunk(qf, kk, g, v, w3, is_query, pair_mask, state_ref, reverse):
    c = HG_CHUNK
    nt = (((1,), (1,)), ((), ()))
    g1 = g.astype(jnp.bfloat16)
    r1 = g - g1.astype(jnp.float32)
    g2 = r1.astype(jnp.bfloat16)
    g3 = (r1 - g2.astype(jnp.float32)).astype(jnp.bfloat16)
    decay = jnp.exp(jnp.dot(w3, jnp.concatenate([g1, g2, g3], axis=0),
                            preferred_element_type=jnp.float32))
    d_in = decay[0:c]
    edge = d_in[0:1, :] if reverse else d_in[c - 1:c, :]
    q_in = (qf * d_in).astype(jnp.bfloat16)
    k_out = (kk * decay[c:2 * c]).astype(jnp.bfloat16)
    xs = [(jnp.where(is_query[l], qf, kk) * decay[(2 + l) * c:(3 + l) * c]).astype(jnp.bfloat16)
          for l in range(len(HG_LEVELS))]
    qk = qf * kk
    vb = v.astype(jnp.bfloat16)

    outs = []
    for h in range(HG_HALF // HG_EXPAND):
        sl = slice(h * HG_EXPAND, (h + 1) * HG_EXPAND)
        a = None
        for l in range(len(HG_LEVELS)):
            x = xs[l][:, sl]
            s_l = pair_mask[l] * lax.dot_general(x, x, nt, preferred_element_type=jnp.float32)
            a = s_l if a is None else a + s_l
        st = state_ref[h]
        oh = (jnp.dot(a.astype(jnp.bfloat16), vb[:, sl], preferred_element_type=jnp.float32)
              + lax.dot_general(q_in[:, sl], st.astype(jnp.bfloat16), nt,
                                preferred_element_type=jnp.float32)
              + jnp.sum(qk[:, sl], axis=-1, keepdims=True) * v[:, sl])
        outs.append(oh)
        upd = lax.dot_general(vb[:, sl], k_out[:, sl], (((0,), (0,)), ((), ())),
                              preferred_element_type=jnp.float32)
        state_ref[h] = st * edge[:, sl] + upd
    return jnp.concatenate(outs, axis=-1)


def _hgrn_kernel(q_ref, f_ref, i_ref, lb_ref, w3_ref, o_ref, state_scr, *, reverse, rows):
    @pl.when(pl.program_id(1) == 0)
    def _():
        state_scr[...] = jnp.zeros(state_scr.shape, jnp.float32)

    lbp = lb_ref[...]
    e = jnp.exp(lbp - jnp.max(lbp, axis=0, keepdims=True))
    lb = e[0:1, :] / jnp.sum(e, axis=0, keepdims=True)
    w3 = w3_ref[...]
    is_query, pair_mask = _hgrn_level_masks(reverse)
    n_chunks = rows // HG_CHUNK
    order = range(n_chunks - 1, -1, -1) if reverse else range(n_chunks)
    for ci in order:
        r = slice(ci * HG_CHUNK, (ci + 1) * HG_CHUNK)
        qf = _silu(q_ref[0, r, :]) * (HG_EXPAND ** -0.5)
        f = lb + (1.0 - lb) * _sigmoid(f_ref[0, r, :])
        o_ref[r, :] = _hgrn_chunk(qf, 1.0 - f, jnp.log(f), i_ref[0, r, :], w3, is_query, pair_mask,
                                  state_scr, reverse)


def _hgrn_direction(proj, lb_dir, reverse, rows):
    s = proj.shape[1]
    nblk = s // rows
    n_half = HG_WIDTH // HG_HALF
    blk = (lambda n: nblk - 1 - n) if reverse else (lambda n: n)
    f_tile = 7 if reverse else 5

    w3 = _hgrn_tables(reverse)

    def tile(t0):
        return pl.BlockSpec((1, rows, COL_TILE), lambda h, n: (t0 + h, blk(n), 0))

    return pl.pallas_call(
        functools.partial(_hgrn_kernel, reverse=reverse, rows=rows),
        grid=(n_half, nblk),
        in_specs=[tile(3), tile(f_tile), tile(9),
                  pl.BlockSpec((lb_dir.shape[0], HG_HALF), lambda h, n: (0, h)),
                  pl.BlockSpec(w3.shape, lambda h, n: (0, 0))],
        out_specs=pl.BlockSpec((rows, HG_HALF), lambda h, n: (blk(n), h)),
        out_shape=jax.ShapeDtypeStruct((s, HG_WIDTH), jnp.float32),
        scratch_shapes=[pltpu.VMEM((HG_HALF // HG_EXPAND, HG_HEAD_V, HG_EXPAND), jnp.float32)],
        compiler_params=_params("parallel", "arbitrary"),
        name="hgrn_bwd" if reverse else "hgrn_fwd",
    )(proj, proj, proj, lb_dir, w3)


def _mix_out_kernel(att_ref, of_ref, ob_ref, g0_ref, g1_ref, hn_ref, w_ref, x_ref, gate_ref,
                    npost_ref, npre_ref, sc_ref, sh_ref, x1_ref, h2_ref):
    hsum = of_ref[...] + ob_ref[...]
    gate = jnp.concatenate([g0_ref[0], g1_ref[0]], axis=-1)
    heads = []
    for h in range(HG_HEADS):
        sl = slice(h * HG_HEAD_V, (h + 1) * HG_HEAD_V)
        heads.append(_rms(hsum[:, sl], hn_ref[:, sl]) * _silu(gate[:, sl]))
    hg = jnp.concatenate(heads, axis=-1).astype(jnp.bfloat16)
    mix = (jnp.dot(att_ref[...], w_ref[0:ATT_WIDTH, :], preferred_element_type=jnp.float32)
           + jnp.dot(hg, w_ref[ATT_WIDTH:, :], preferred_element_type=jnp.float32))
    x1 = x_ref[...] + gate_ref[...] * _rms(mix, npost_ref[...])
    x1_ref[...] = x1
    h2 = _rms(x1, npre_ref[...]) * (1.0 + sc_ref[...]) + sh_ref[...]
    h2_ref[...] = h2.astype(jnp.bfloat16)


def _mix_out(o_att, o_f, o_b, proj, hn, w_out_bf16, x2, g1, npost, npre, sc2, sh2, tm):
    s, d = x2.shape
    vec = lambda n: pl.BlockSpec((1, n), lambda i: (0, 0))
    row = lambda n: pl.BlockSpec((tm, n), lambda i: (i, 0))
    gtile = lambda t: pl.BlockSpec((1, tm, COL_TILE), lambda i, t=t: (t, i, 0))
    return pl.pallas_call(
        _mix_out_kernel,
        grid=(s // tm,),
        in_specs=[row(ATT_WIDTH), row(HG_WIDTH), row(HG_WIDTH), gtile(11), gtile(12),
                  vec(HG_WIDTH), pl.BlockSpec((d, d), lambda i: (0, 0)), row(d),
                  vec(d), vec(d), vec(d), vec(d), vec(d)],
        out_specs=[row(d), row(d)],
        out_shape=[jax.ShapeDtypeStruct((s, d), jnp.float32),
                   jax.ShapeDtypeStruct((s, d), jnp.bfloat16)],
        compiler_params=_params("parallel"),
        name="mix_out",
    )(o_att, o_f, o_b, proj, proj, hn, w_out_bf16, x2, g1, npost, npre, sc2, sh2)


def _gelu_tanh(x):
    return 0.5 * x * (1.0 + jnp.tanh(math.sqrt(2.0 / math.pi) * (x + 0.044715 * (x * x * x))))


def _ffn_kernel(h_ref, hp_ref, hn_ref, wa_ref, wb_ref, cwa_ref, cwb_ref, cba_ref, cbb_ref,
                wd_ref, x1_ref, gate_ref, npost_ref, o_ref, hext_scr, *, tm, tf):
    i, j = pl.program_id(0), pl.program_id(1)
    halo = BF16_SUBLANES
    n_ext = tm + 2 * halo

    @pl.when(j == 0)
    def _():
        hext_scr[0:halo, :] = jnp.where(i == 0, jnp.zeros_like(hp_ref[...]), hp_ref[...])
        hext_scr[halo:halo + tm, :] = h_ref[...]
        hext_scr[halo + tm:, :] = jnp.where(i == pl.num_programs(0) - 1,
                                            jnp.zeros_like(hn_ref[...]), hn_ref[...])
        o_ref[...] = jnp.zeros(o_ref.shape, o_ref.dtype)

    h_ext = hext_scr[...]

    def conv(u, cw, cb):
        u_prev = pltpu.roll(u, 1, axis=0)[halo:halo + tm]
        u_next = pltpu.roll(u, n_ext - 1, axis=0)[halo:halo + tm]
        return cw[0:1, :] * u_prev + cw[1:2, :] * u[halo:halo + tm] + cw[2:3, :] * u_next + cb

    groups = [slice(c0, c0 + FFN_COL_GROUP) for c0 in range(0, tf, FFN_COL_GROUP)]
    ups = [(jnp.dot(h_ext, wa_ref[:, cols], preferred_element_type=jnp.float32),
            jnp.dot(h_ext, wb_ref[:, cols], preferred_element_type=jnp.float32)) for cols in groups]
    for cols, (ua, ub) in zip(groups, ups):
        a = conv(ua, cwa_ref[:, cols], cba_ref[:, cols])
        b = conv(ub, cwb_ref[:, cols], cbb_ref[:, cols])
        act = (_gelu_tanh(a) * b).astype(jnp.bfloat16)
        o_ref[...] += jnp.dot(act, wd_ref[cols, :], preferred_element_type=jnp.float32)

    @pl.when(j == pl.num_programs(1) - 1)
    def _():
        o_ref[...] = x1_ref[...] + gate_ref[...] * _rms(o_ref[...], npost_ref[...])


def _ffn(h2, w_up_bf16, conv_w, conv_b, w_down_bf16, x1, g2, npost, tm, tf):
    s, d = x1.shape
    nf = D_FF // tf
    hb = tm // BF16_SUBLANES
    n_hblk = s // BF16_SUBLANES
    vec = pl.BlockSpec((1, d), lambda i, j: (0, 0))
    single = pl.Buffered(1)
    return pl.pallas_call(
        functools.partial(_ffn_kernel, tm=tm, tf=tf),
        grid=(s // tm, nf),
        in_specs=[pl.BlockSpec((tm, d), lambda i, j: (i, 0), pipeline_mode=single),
                  pl.BlockSpec((BF16_SUBLANES, d), lambda i, j: (jnp.maximum(i * hb - 1, 0), 0)),
                  pl.BlockSpec((BF16_SUBLANES, d),
                               lambda i, j: (jnp.minimum((i + 1) * hb, n_hblk - 1), 0)),
                  pl.BlockSpec((d, tf), lambda i, j: (0, j)),
                  pl.BlockSpec((d, tf), lambda i, j: (0, j + nf)),
                  pl.BlockSpec((3, tf), lambda i, j: (0, j)),
                  pl.BlockSpec((3, tf), lambda i, j: (0, j + nf)),
                  pl.BlockSpec((1, tf), lambda i, j: (0, j)),
                  pl.BlockSpec((1, tf), lambda i, j: (0, j + nf)),
                  pl.BlockSpec((tf, d), lambda i, j: (j, 0)),
                  pl.BlockSpec((tm, d), lambda i, j: (i, 0), pipeline_mode=single), vec, vec],
        out_specs=pl.BlockSpec((tm, d), lambda i, j: (i, 0)),
        out_shape=jax.ShapeDtypeStruct((s, d), jnp.float32),
        scratch_shapes=[pltpu.VMEM((tm + 2 * BF16_SUBLANES, d), jnp.bfloat16)],
        compiler_params=pltpu.CompilerParams(dimension_semantics=("parallel", "arbitrary"),
                                             vmem_limit_bytes=FFN_VMEM_LIMIT),
        name="conv_ffn",
    )(h2, h2, h2, w_up_bf16, w_up_bf16, conv_w, conv_w, conv_b, conv_b, w_down_bf16, x1, g2, npost)


def _rope_tables(s):
    rows = s // GRID_W
    t = jnp.arange(s)
    r = (t // GRID_W - rows // 2).astype(jnp.float32)
    cpos = (t % GRID_W - GRID_W // 2).astype(jnp.float32)
    axis_dim = HEAD_DIM // 2
    inv = ROPE_THETA ** (-(2.0 * jnp.arange(axis_dim // 2, dtype=jnp.float32)) / axis_dim)
    ang_r = r[:, None] * inv[None, :]
    ang_c = cpos[:, None] * inv[None, :]
    ang = jnp.concatenate([ang_r, ang_r, ang_c, ang_c], axis=-1)
    cos, sin = jnp.cos(ang), jnp.sin(ang)
    low = (jnp.arange(HEAD_DIM) % axis_dim) < (axis_dim // 2)
    return cos, jnp.where(low, -sin, 0.0), jnp.where(low, 0.0, sin)


def kernel(x, c, w_ada, b_ada, norm_mix_pre, norm_mix_post, w_in, q_norm, k_norm, hg_lower_bound,
           hg_out_norm, w_out, norm_ffn_pre, norm_ffn_post, w_up, conv_w, conv_b, w_down):
    batch, s, d = x.shape
    assert batch == 1 and d == D_MODEL and s % GRID_W == 0
    layer = 0
    x2 = x.reshape(s, d)
    tm_big = min(1024, s)
    tm_mid = min(512, s)
    tm_small = min(256, s)

    mod = _modulation(c.reshape(d, 1), w_ada[layer], b_ada[layer].reshape(1, -1))
    sh1, sc1, g1, sh2, sc2, g2 = [mod[:, k * d:(k + 1) * d] for k in range(6)]

    proj = _in_proj(x2, norm_mix_pre[layer].reshape(1, d), sc1, sh1,
                    w_in[layer].astype(jnp.bfloat16), tm_big)

    cos, sin_a, sin_b = _rope_tables(s)
    qt, kt, vt = _att_prep(proj, cos, sin_a, sin_b, q_norm[layer].reshape(1, -1),
                           k_norm[layer].reshape(1, -1), tm_mid)
    o_att = _attention(qt, kt, vt, tq=min(512, s), tk=min(512, s))

    rows = min(256, s)
    o_f = _hgrn_direction(proj, hg_lower_bound[:, 0, :], False, rows)
    o_b = _hgrn_direction(proj, hg_lower_bound[:, 1, :], True, rows)

    x1, h2 = _mix_out(o_att, o_f, o_b, proj, hg_out_norm[layer].reshape(1, -1),
                      w_out[layer].astype(jnp.bfloat16), x2, g1,
                      norm_mix_post[layer].reshape(1, d), norm_ffn_pre[layer].reshape(1, d),
                      sc2, sh2, tm_small)

    out = _ffn(h2, w_up[layer].astype(jnp.bfloat16), conv_w[layer], conv_b[layer].reshape(1, -1),
               w_down[layer].astype(jnp.bfloat16), x1, g2, norm_ffn_post[layer].reshape(1, d),
               tm_big, 512)
    return out.reshape(batch, s, d)
```

```python
import functools
import math

import numpy as np
import jax
import jax.numpy as jnp
from jax import lax
from jax.experimental import pallas as pl
from jax.experimental.pallas import tpu as pltpu

D_MODEL = 2048
GRID_W = 64
ATT_HEADS = 8
ATT_KV_HEADS = 2
ATT_GROUP = ATT_HEADS // ATT_KV_HEADS
HEAD_DIM = 128
ATT_WIDTH = ATT_HEADS * HEAD_DIM
KV_WIDTH = ATT_KV_HEADS * HEAD_DIM
ROPE_THETA = 10000.0
HG_WIDTH = 1024
HG_HEADS = 8
HG_HEAD_V = 128
HG_EXPAND = 128
IN_COLS = ATT_WIDTH + 2 * KV_WIDTH + 5 * HG_WIDTH
D_FF = 5632
EPS = 1e-6
LOG2E = math.log2(math.e)

LANES = 128
BF16_SUBLANES = 16
VMEM_LIMIT = 56 * 1024 * 1024
FFN_VMEM_LIMIT = 60 * 1024 * 1024

COL_TILE = 512
N_COL_TILES = IN_COLS // COL_TILE
HG_CHUNK = 64
HG_LEVELS = (32, 16, 8, 4, 2, 1)
HG_HALF = 512
ATT_ROW_BLOCK = 128
NORM_ROW_BLOCK = 64
FFN_COL_GROUP = 256


def _params(*sem):
    return pltpu.CompilerParams(dimension_semantics=sem, vmem_limit_bytes=VMEM_LIMIT)


def _rms(x, w):
    return x * lax.rsqrt(jnp.mean(x * x, axis=-1, keepdims=True) + EPS) * w


def _sigmoid(x):
    return 1.0 / (1.0 + jnp.exp(-x))


def _silu(x):
    return x * _sigmoid(x)


def _mod_kernel(c_ref, w_ref, b_ref, o_ref):
    s = _silu(c_ref[...])
    o_ref[...] = jnp.sum(s * w_ref[...], axis=0, keepdims=True) + b_ref[...]


def _modulation(c_col, w_ada, b_ada):
    d, n = w_ada.shape
    tn = 1024
    return pl.pallas_call(
        _mod_kernel,
        grid=(n // tn,),
        in_specs=[pl.BlockSpec((d, 1), lambda j: (0, 0)),
                  pl.BlockSpec((d, tn), lambda j: (0, j)),
                  pl.BlockSpec((1, tn), lambda j: (0, j))],
        out_specs=pl.BlockSpec((1, tn), lambda j: (0, j)),
        out_shape=jax.ShapeDtypeStruct((1, n), jnp.float32),
        compiler_params=_params("arbitrary"),
        name="adaln_mod",
    )(c_col, w_ada, b_ada)


def _in_proj_kernel(x_ref, nw_ref, sc_ref, sh_ref, w_ref, o_ref, h_scr):
    @pl.when(pl.program_id(1) == 0)
    def _():
        for r0 in range(0, x_ref.shape[0], NORM_ROW_BLOCK):
            rows = slice(r0, r0 + NORM_ROW_BLOCK)
            h = _rms(x_ref[rows, :], nw_ref[...]) * (1.0 + sc_ref[...]) + sh_ref[...]
            h_scr[rows, :] = h.astype(jnp.bfloat16)

    o_ref[0] = jnp.dot(h_scr[...], w_ref[...], preferred_element_type=jnp.float32)


def _in_proj(x2, nw, sc, sh, w_in_bf16, tm):
    s, d = x2.shape
    vec = pl.BlockSpec((1, d), lambda i, j: (0, 0))
    return pl.pallas_call(
        _in_proj_kernel,
        grid=(s // tm, N_COL_TILES),
        in_specs=[pl.BlockSpec((tm, d), lambda i, j: (i, 0)), vec, vec, vec,
                  pl.BlockSpec((d, COL_TILE), lambda i, j: (0, j))],
        out_specs=pl.BlockSpec((1, tm, COL_TILE), lambda i, j: (j, i, 0)),
        out_shape=jax.ShapeDtypeStruct((N_COL_TILES, s, COL_TILE), jnp.float32),
        scratch_shapes=[pltpu.VMEM((tm, d), jnp.bfloat16)],
        compiler_params=_params("parallel", "arbitrary"),
        name="in_proj",
    )(x2, nw, sc, sh, w_in_bf16)


def _rope(y, cos, sin_a, sin_b):
    return (y * cos + pltpu.roll(y, LANES - 32, axis=1) * sin_a
            + pltpu.roll(y, 32, axis=1) * sin_b)


def _att_prep_kernel(q0_ref, q1_ref, kv_ref, cos_ref, sa_ref, sb_ref, qn_ref, kn_ref,
                     q_out, k_out, v_out):
    cos, sa, sb = cos_ref[...], sa_ref[...], sb_ref[...]
    scale = HEAD_DIM ** -0.5 * LOG2E
    for t, src in enumerate((q0_ref, q1_ref)):
        for hh in range(COL_TILE // HEAD_DIM):
            xh = src[0, :, hh * HEAD_DIM:(hh + 1) * HEAD_DIM]
            y = _rope(_rms(xh, qn_ref[...]), cos, sa, sb) * scale
            col = t * COL_TILE + hh * HEAD_DIM
            q_out[:, col:col + HEAD_DIM] = y.astype(jnp.bfloat16)
    for hh in range(ATT_KV_HEADS):
        xh = kv_ref[0, :, hh * HEAD_DIM:(hh + 1) * HEAD_DIM]
        y = _rope(_rms(xh, kn_ref[...]), cos, sa, sb)
        k_out[:, hh * HEAD_DIM:(hh + 1) * HEAD_DIM] = y.astype(jnp.bfloat16)
    ones = jnp.ones((v_out.shape[0], HEAD_DIM), jnp.bfloat16)
    for hh in range(ATT_KV_HEADS):
        vh = kv_ref[0, :, KV_WIDTH + hh * HEAD_DIM:KV_WIDTH + (hh + 1) * HEAD_DIM]
        v_out[:, 2 * hh * HEAD_DIM:(2 * hh + 1) * HEAD_DIM] = vh.astype(jnp.bfloat16)
        v_out[:, (2 * hh + 1) * HEAD_DIM:(2 * hh + 2) * HEAD_DIM] = ones


def _att_prep(proj, cos, sin_a, sin_b, qn, kn, tm):
    s = proj.shape[1]
    tile = lambda t: pl.BlockSpec((1, tm, COL_TILE), lambda i, t=t: (t, i, 0))
    tab = pl.BlockSpec((tm, HEAD_DIM), lambda i: (i, 0))
    vec = pl.BlockSpec((1, HEAD_DIM), lambda i: (0, 0))
    return pl.pallas_call(
        _att_prep_kernel,
        grid=(s // tm,),
        in_specs=[tile(0), tile(1), tile(2), tab, tab, tab, vec, vec],
        out_specs=[pl.BlockSpec((tm, ATT_WIDTH), lambda i: (i, 0)),
                   pl.BlockSpec((tm, KV_WIDTH), lambda i: (i, 0)),
                   pl.BlockSpec((tm, 2 * KV_WIDTH), lambda i: (i, 0))],
        out_shape=[jax.ShapeDtypeStruct((s, ATT_WIDTH), jnp.bfloat16),
                   jax.ShapeDtypeStruct((s, KV_WIDTH), jnp.bfloat16),
                   jax.ShapeDtypeStruct((s, 2 * KV_WIDTH), jnp.bfloat16)],
        compiler_params=_params("parallel"),
        name="att_prep",
    )(proj, proj, proj, cos, sin_a, sin_b, qn, kn)


def _attention_kernel(q_ref, k_ref, v_ref, o_ref, m_scr, acc_scr, alpha_scr, s_scr, p_scr, *,
                      tq, tk):
    n_kv = k_ref.shape[0] // tk
    m_rows = ATT_GROUP * tq
    q = jnp.concatenate([q_ref[:, g * HEAD_DIM:(g + 1) * HEAD_DIM] for g in range(ATT_GROUP)],
                        axis=0)
    m_scr[...] = jnp.full(m_scr.shape, -jnp.inf, jnp.float32)
    acc_scr[...] = jnp.zeros(acc_scr.shape, jnp.float32)

    def scores(j):
        off = pl.multiple_of(j * tk, tk)
        return lax.dot_general(q, k_ref[pl.ds(off, tk), :], (((1,), (1,)), ((), ())),
                               preferred_element_type=jnp.float32)

    def softmax_pv(slot, j):
        for r0 in range(0, m_rows, ATT_ROW_BLOCK):
            rows = slice(r0, r0 + ATT_ROW_BLOCK)
            s = s_scr[slot, rows, :]
            m_old = m_scr[rows, :]
            m_new = jnp.maximum(m_old, jnp.max(s, axis=-1, keepdims=True))
            alpha_scr[rows, :] = jnp.exp2(m_old - m_new)
            m_scr[rows, :] = m_new
            p_scr[rows, :] = jnp.concatenate(
                [jnp.exp2(s[:, t * LANES:(t + 1) * LANES] - m_new) for t in range(tk // LANES)],
                axis=-1).astype(jnp.bfloat16)
        off = pl.multiple_of(j * tk, tk)
        pv = jnp.dot(p_scr[...], v_ref[pl.ds(off, tk), :], preferred_element_type=jnp.float32)
        alpha = alpha_scr[...]
        acc_scr[...] = jnp.concatenate([alpha, alpha], axis=-1) * acc_scr[...] + pv

    s_scr[0] = scores(0)

    def body(jj, carry):
        j = 2 * jj
        s_scr[1] = scores(j + 1)
        softmax_pv(0, j)
        s_scr[0] = scores(jnp.minimum(j + 2, n_kv - 1))
        softmax_pv(1, j + 1)
        return carry

    lax.fori_loop(0, n_kv // 2, body, 0)
    out = acc_scr[:, 0:HEAD_DIM] / acc_scr[:, HEAD_DIM:2 * HEAD_DIM]
    for g in range(ATT_GROUP):
        o_ref[:, g * HEAD_DIM:(g + 1) * HEAD_DIM] = out[g * tq:(g + 1) * tq].astype(o_ref.dtype)


def _attention(qt, kt, vt, tq, tk):
    s = qt.shape[0]
    gw = ATT_GROUP * HEAD_DIM
    return pl.pallas_call(
        functools.partial(_attention_kernel, tq=tq, tk=tk),
        grid=(ATT_KV_HEADS, s // tq),
        in_specs=[pl.BlockSpec((tq, gw), lambda h, i: (i, h)),
                  pl.BlockSpec((s, HEAD_DIM), lambda h, i: (0, h)),
                  pl.BlockSpec((s, 2 * HEAD_DIM), lambda h, i: (0, h))],
        out_specs=pl.BlockSpec((tq, gw), lambda h, i: (i, h)),
        out_shape=jax.ShapeDtypeStruct((s, ATT_WIDTH), jnp.bfloat16),
        scratch_shapes=[pltpu.VMEM((ATT_GROUP * tq, LANES), jnp.float32),
                        pltpu.VMEM((ATT_GROUP * tq, 2 * HEAD_DIM), jnp.float32),
                        pltpu.VMEM((ATT_GROUP * tq, LANES), jnp.float32),
                        pltpu.VMEM((2, ATT_GROUP * tq, tk), jnp.float32),
                        pltpu.VMEM((ATT_GROUP * tq, tk), jnp.bfloat16)],
        compiler_params=_params("parallel", "arbitrary"),
        name="attention",
    )(qt, kt, vt)


def _hgrn_tables(reverse):
    c = HG_CHUNK
    idx = np.arange(c)
    row, u = idx[:, None], idx[None, :]
    blocks = [(u >= row) if reverse else (u <= row), (u < row) if reverse else (u > row)]
    w = np.concatenate(blocks, axis=0).astype(np.float32)
    return jnp.asarray(np.tile(w, (1, 3)), jnp.bfloat16)


def _hgrn_level_masks(reverse):
    c = HG_CHUNK
    row = lax.broadcasted_iota(jnp.int32, (c, c), 0)
    col = lax.broadcasted_iota(jnp.int32, (c, c), 1)
    is_query, pair_mask = [], []
    for h in HG_LEVELS:
        q_row = (row % (2 * h) < h) if reverse else (row % (2 * h) >= h)
        q_col = (col % (2 * h) < h) if reverse else (col % (2 * h) >= h)
        same = (row // (2 * h)) == (col // (2 * h))
        pair_mask.append((same & q_row & jnp.logical_not(q_col)).astype(jnp.float32))
        is_query.append(q_row[:, 0:1])
    return is_query, pair_mask


def _hgrn_chunk(qf, kk, g, v, w3, is_query, pair_mask, state_ref, reverse):
    c = HG_CHUNK
    nt = (((1,), (1,)), ((), ()))
    g1 = g.astype(jnp.bfloat16)
    r1 = g - g1.astype(jnp.float32)
    g2 = r1.astype(jnp.bfloat16)
    g3 = (r1 - g2.astype(jnp.float32)).astype(jnp.bfloat16)
    sums = jnp.dot(w3, jnp.concatenate([g1, g2, g3], axis=0),
                   preferred_element_type=jnp.float32)
    b = sums[0:c]
    d_in = jnp.exp2(b)
    edge = d_in[0:1, :] if reverse else d_in[c - 1:c, :]
    q_in = (qf * d_in).astype(jnp.bfloat16)
    k_out = (kk * jnp.exp2(sums[c:2 * c])).astype(jnp.bfloat16)
    g_next = pltpu.roll(g, c - 1, axis=0)
    g_prev = pltpu.roll(g, 1, axis=0)
    pos4 = lax.broadcasted_iota(jnp.int32, (c, 1), 0) % 4
    sign = [jnp.where(q, 1.0, -1.0) for q in is_query]
    xs = []
    for l, h in enumerate(HG_LEVELS):
        if h >= 4:
            ref = h if reverse else h - 1
            b3 = b.reshape(c // (2 * h), 2 * h, b.shape[-1])
            bref = jnp.broadcast_to(b3[:, ref:ref + 1, :], b3.shape).reshape(b.shape)
            e_l = (b - bref) * sign[l]
        elif h == 2:
            zero = jnp.zeros_like(g)
            if reverse:
                e_l = jnp.where(pos4 == 0, g + g_next, jnp.where(pos4 == 1, g,
                                                                  jnp.where(pos4 == 2, zero, g_prev)))
            else:
                e_l = jnp.where(pos4 == 0, g_next, jnp.where(pos4 == 1, zero,
                                                             jnp.where(pos4 == 2, g, g + g_prev)))
        else:
            e_l = jnp.where(is_query[l], g, jnp.zeros_like(g))
        xs.append((jnp.where(is_query[l], qf, kk) * jnp.exp2(e_l)).astype(jnp.bfloat16))
    qk = qf * kk
    vb = v.astype(jnp.bfloat16)

    outs = []
    for h in range(HG_HALF // HG_EXPAND):
        sl = slice(h * HG_EXPAND, (h + 1) * HG_EXPAND)
        a = None
        for l in range(len(HG_LEVELS)):
            x = xs[l][:, sl]
            s_l = pair_mask[l] * lax.dot_general(x, x, nt, preferred_element_type=jnp.float32)
            a = s_l if a is None else a + s_l
        st = state_ref[h]
        oh = (jnp.dot(a.astype(jnp.bfloat16), vb[:, sl], preferred_element_type=jnp.float32)
              + lax.dot_general(q_in[:, sl], st.astype(jnp.bfloat16), nt,
                                preferred_element_type=jnp.float32)
              + jnp.sum(qk[:, sl], axis=-1, keepdims=True) * v[:, sl])
        outs.append(oh)
        upd = lax.dot_general(vb[:, sl], k_out[:, sl], (((0,), (0,)), ((), ())),
                              preferred_element_type=jnp.float32)
        state_ref[h] = st * edge[:, sl] + upd
    return jnp.concatenate(outs, axis=-1)


def _hgrn_kernel(qf_ref, qb_ref, ff_ref, fb_ref, if_ref, ib_ref, lbf_ref, lbb_ref, w3f_ref, w3b_ref,
                 of_ref, ob_ref, state_scr, *, rows):
    @pl.when(pl.program_id(1) == 0)
    def _():
        state_scr[...] = jnp.zeros(state_scr.shape, jnp.float32)

    def lower_bound(lb_ref):
        lbp = lb_ref[...]
        e = jnp.exp(lbp - jnp.max(lbp, axis=0, keepdims=True))
        return e[0:1, :] / jnp.sum(e, axis=0, keepdims=True)

    dirs = [(False, qf_ref, ff_ref, if_ref, of_ref, lower_bound(lbf_ref), w3f_ref[...]),
            (True, qb_ref, fb_ref, ib_ref, ob_ref, lower_bound(lbb_ref), w3b_ref[...])]
    masks = [_hgrn_level_masks(False), _hgrn_level_masks(True)]
    n_chunks = rows // HG_CHUNK
    for step in range(n_chunks):
        for d, (reverse, q_ref, f_ref, i_ref, o_ref, lb, w3) in enumerate(dirs):
            ci = n_chunks - 1 - step if reverse else step
            r = slice(ci * HG_CHUNK, (ci + 1) * HG_CHUNK)
            qf = _silu(q_ref[0, r, :]) * (HG_EXPAND ** -0.5)
            f = lb + (1.0 - lb) * _sigmoid(f_ref[0, r, :])
            o_ref[r, :] = _hgrn_chunk(qf, 1.0 - f, jnp.log2(f), i_ref[0, r, :], w3, masks[d][0],
                                      masks[d][1], state_scr.at[d], reverse)


def _hgrn(proj, lb_fwd, lb_bwd, rows):
    s = proj.shape[1]
    nblk = s // rows
    n_half = HG_WIDTH // HG_HALF
    w3f, w3b = _hgrn_tables(False), _hgrn_tables(True)
    fwd = lambda n: n
    bwd = lambda n: nblk - 1 - n

    def tile(t0, blk):
        return pl.BlockSpec((1, rows, COL_TILE), lambda h, n: (t0 + h, blk(n), 0))

    lb_spec = pl.BlockSpec((lb_fwd.shape[0], HG_HALF), lambda h, n: (0, h))
    w3_spec = pl.BlockSpec(w3f.shape, lambda h, n: (0, 0))
    out = jax.ShapeDtypeStruct((s, HG_WIDTH), jnp.float32)
    return pl.pallas_call(
        functools.partial(_hgrn_kernel, rows=rows),
        grid=(n_half, nblk),
        in_specs=[tile(3, fwd), tile(3, bwd), tile(5, fwd), tile(7, bwd), tile(9, fwd), tile(9, bwd),
                  lb_spec, lb_spec, w3_spec, w3_spec],
        out_specs=[pl.BlockSpec((rows, HG_HALF), lambda h, n: (fwd(n), h)),
                   pl.BlockSpec((rows, HG_HALF), lambda h, n: (bwd(n), h))],
        out_shape=[out, out],
        scratch_shapes=[pltpu.VMEM((2, HG_HALF // HG_EXPAND, HG_HEAD_V, HG_EXPAND), jnp.float32)],
        compiler_params=_params("parallel", "arbitrary"),
        name="hgrn",
    )(proj, proj, proj, proj, proj, proj, lb_fwd, lb_bwd, w3f, w3b)


def _mix_out_kernel(att_ref, of_ref, ob_ref, g0_ref, g1_ref, hn_ref, w_ref, x_ref, gate_ref,
                    npost_ref, npre_ref, sc_ref, sh_ref, x1_ref, h2_ref):
    hsum = of_ref[...] + ob_ref[...]
    gate = jnp.concatenate([g0_ref[0], g1_ref[0]], axis=-1)
    heads = []
    for h in range(HG_HEADS):
        sl = slice(h * HG_HEAD_V, (h + 1) * HG_HEAD_V)
        heads.append(_rms(hsum[:, sl], hn_ref[:, sl]) * _silu(gate[:, sl]))
    hg = jnp.concatenate(heads, axis=-1).astype(jnp.bfloat16)
    mix = (jnp.dot(att_ref[...], w_ref[0:ATT_WIDTH, :], preferred_element_type=jnp.float32)
           + jnp.dot(hg, w_ref[ATT_WIDTH:, :], preferred_element_type=jnp.float32))
    x1 = x_ref[...] + gate_ref[...] * _rms(mix, npost_ref[...])
    x1_ref[...] = x1
    h2 = _rms(x1, npre_ref[...]) * (1.0 + sc_ref[...]) + sh_ref[...]
    h2_ref[...] = h2.astype(jnp.bfloat16)


def _mix_out(o_att, o_f, o_b, proj, hn, w_out_bf16, x2, g1, npost, npre, sc2, sh2, tm):
    s, d = x2.shape
    vec = lambda n: pl.BlockSpec((1, n), lambda i: (0, 0))
    row = lambda n: pl.BlockSpec((tm, n), lambda i: (i, 0))
    gtile = lambda t: pl.BlockSpec((1, tm, COL_TILE), lambda i, t=t: (t, i, 0))
    return pl.pallas_call(
        _mix_out_kernel,
        grid=(s // tm,),
        in_specs=[row(ATT_WIDTH), row(HG_WIDTH), row(HG_WIDTH), gtile(11), gtile(12),
                  vec(HG_WIDTH), pl.BlockSpec((d, d), lambda i: (0, 0)), row(d),
                  vec(d), vec(d), vec(d), vec(d), vec(d)],
        out_specs=[row(d), row(d)],
        out_shape=[jax.ShapeDtypeStruct((s, d), jnp.float32),
                   jax.ShapeDtypeStruct((s, d), jnp.bfloat16)],
        compiler_params=_params("parallel"),
        name="mix_out",
    )(o_att, o_f, o_b, proj, proj, hn, w_out_bf16, x2, g1, npost, npre, sc2, sh2)


def _gelu_tanh(x):
    return 0.5 * x * (1.0 + jnp.tanh(math.sqrt(2.0 / math.pi) * (x + 0.044715 * (x * x * x))))


def _ffn_kernel(h_ref, hp_ref, hn_ref, wa_ref, wb_ref, cwa_ref, cwb_ref, cba_ref, cbb_ref,
                wd_ref, x1_ref, gate_ref, npost_ref, o_ref, hext_scr, *, tm, tf):
    i, j = pl.program_id(0), pl.program_id(1)
    halo = BF16_SUBLANES
    n_ext = tm + 2 * halo

    @pl.when(j == 0)
    def _():
        hext_scr[0:halo, :] = jnp.where(i == 0, jnp.zeros_like(hp_ref[...]), hp_ref[...])
        hext_scr[halo:halo + tm, :] = h_ref[...]
        hext_scr[halo + tm:, :] = jnp.where(i == pl.num_programs(0) - 1,
                                            jnp.zeros_like(hn_ref[...]), hn_ref[...])
        o_ref[...] = jnp.zeros(o_ref.shape, o_ref.dtype)

    h_ext = hext_scr[...]

    def conv(u, cw, cb):
        u_prev = pltpu.roll(u, 1, axis=0)[halo:halo + tm]
        u_next = pltpu.roll(u, n_ext - 1, axis=0)[halo:halo + tm]
        return cw[0:1, :] * u_prev + cw[1:2, :] * u[halo:halo + tm] + cw[2:3, :] * u_next + cb

    groups = [slice(c0, c0 + FFN_COL_GROUP) for c0 in range(0, tf, FFN_COL_GROUP)]
    ups = [(jnp.dot(h_ext, wa_ref[:, cols], preferred_element_type=jnp.float32),
            jnp.dot(h_ext, wb_ref[:, cols], preferred_element_type=jnp.float32)) for cols in groups]
    for cols, (ua, ub) in zip(groups, ups):
        a = conv(ua, cwa_ref[:, cols], cba_ref[:, cols])
        b = conv(ub, cwb_ref[:, cols], cbb_ref[:, cols])
        act = (_gelu_tanh(a) * b).astype(jnp.bfloat16)
        o_ref[...] += jnp.dot(act, wd_ref[cols, :], preferred_element_type=jnp.float32)

    @pl.when(j == pl.num_programs(1) - 1)
    def _():
        o_ref[...] = x1_ref[...] + gate_ref[...] * _rms(o_ref[...], npost_ref[...])


def _ffn(h2, w_up_bf16, conv_w, conv_b, w_down_bf16, x1, g2, npost, tm, tf):
    s, d = x1.shape
    nf = D_FF // tf
    hb = tm // BF16_SUBLANES
    n_hblk = s // BF16_SUBLANES
    vec = pl.BlockSpec((1, d), lambda i, j: (0, 0))
    single = pl.Buffered(1)
    return pl.pallas_call(
        functools.partial(_ffn_kernel, tm=tm, tf=tf),
        grid=(s // tm, nf),
        in_specs=[pl.BlockSpec((tm, d), lambda i, j: (i, 0), pipeline_mode=single),
                  pl.BlockSpec((BF16_SUBLANES, d), lambda i, j: (jnp.maximum(i * hb - 1, 0), 0)),
                  pl.BlockSpec((BF16_SUBLANES, d),
                               lambda i, j: (jnp.minimum((i + 1) * hb, n_hblk - 1), 0)),
                  pl.BlockSpec((d, tf), lambda i, j: (0, j)),
                  pl.BlockSpec((d, tf), lambda i, j: (0, j + nf)),
                  pl.BlockSpec((3, tf), lambda i, j: (0, j)),
                  pl.BlockSpec((3, tf), lambda i, j: (0, j + nf)),
                  pl.BlockSpec((1, tf), lambda i, j: (0, j)),
                  pl.BlockSpec((1, tf), lambda i, j: (0, j + nf)),
                  pl.BlockSpec((tf, d), lambda i, j: (j, 0)),
                  pl.BlockSpec((tm, d), lambda i, j: (i, 0), pipeline_mode=single), vec, vec],
        out_specs=pl.BlockSpec((tm, d), lambda i, j: (i, 0)),
        out_shape=jax.ShapeDtypeStruct((s, d), jnp.float32),
        scratch_shapes=[pltpu.VMEM((tm + 2 * BF16_SUBLANES, d), jnp.bfloat16)],
        compiler_params=pltpu.CompilerParams(dimension_semantics=("parallel", "arbitrary"),
                                             vmem_limit_bytes=FFN_VMEM_LIMIT),
        name="conv_ffn",
    )(h2, h2, h2, w_up_bf16, w_up_bf16, conv_w, conv_w, conv_b, conv_b, w_down_bf16, x1, g2, npost)


def _rope_tables(s):
    rows = s // GRID_W
    t = jnp.arange(s)
    r = (t // GRID_W - rows // 2).astype(jnp.float32)
    cpos = (t % GRID_W - GRID_W // 2).astype(jnp.float32)
    axis_dim = HEAD_DIM // 2
    inv = ROPE_THETA ** (-(2.0 * jnp.arange(axis_dim // 2, dtype=jnp.float32)) / axis_dim)
    ang_r = r[:, None] * inv[None, :]
    ang_c = cpos[:, None] * inv[None, :]
    ang = jnp.concatenate([ang_r, ang_r, ang_c, ang_c], axis=-1)
    cos, sin = jnp.cos(ang), jnp.sin(ang)
    low = (jnp.arange(HEAD_DIM) % axis_dim) < (axis_dim // 2)
    return cos, jnp.where(low, -sin, 0.0), jnp.where(low, 0.0, sin)


def kernel(x, c, w_ada, b_ada, norm_mix_pre, norm_mix_post, w_in, q_norm, k_norm, hg_lower_bound,
           hg_out_norm, w_out, norm_ffn_pre, norm_ffn_post, w_up, conv_w, conv_b, w_down):
    batch, s, d = x.shape
    assert batch == 1 and d == D_MODEL and s % GRID_W == 0
    layer = 0
    x2 = x.reshape(s, d)
    tm_big = min(1024, s)
    tm_mid = min(512, s)
    tm_small = min(256, s)

    mod = _modulation(c.reshape(d, 1), w_ada[layer], b_ada[layer].reshape(1, -1))
    sh1, sc1, g1, sh2, sc2, g2 = [mod[:, k * d:(k + 1) * d] for k in range(6)]

    proj = _in_proj(x2, norm_mix_pre[layer].reshape(1, d), sc1, sh1,
                    w_in[layer].astype(jnp.bfloat16), tm_big)

    cos, sin_a, sin_b = _rope_tables(s)
    qt, kt, vt = _att_prep(proj, cos, sin_a, sin_b, q_norm[layer].reshape(1, -1),
                           k_norm[layer].reshape(1, -1), tm_mid)
    o_att = _attention(qt, kt, vt, tq=min(512, s), tk=min(512, s))

    rows = min(256, s)
    o_f, o_b = _hgrn(proj, hg_lower_bound[:, 0, :], hg_lower_bound[:, 1, :], rows)

    x1, h2 = _mix_out(o_att, o_f, o_b, proj, hg_out_norm[layer].reshape(1, -1),
                      w_out[layer].astype(jnp.bfloat16), x2, g1,
                      norm_mix_post[layer].reshape(1, d), norm_ffn_pre[layer].reshape(1, d),
                      sc2, sh2, tm_small)

    out = _ffn(h2, w_up[layer].astype(jnp.bfloat16), conv_w[layer], conv_b[layer].reshape(1, -1),
               w_down[layer].astype(jnp.bfloat16), x1, g2, norm_ffn_post[layer].reshape(1, d),
               tm_big, 512)
    return out.reshape(batch, s, d)
```

```python
import functools
import math

import numpy as np
import jax
import jax.numpy as jnp
from jax import lax
from jax.experimental import pallas as pl
from jax.experimental.pallas import tpu as pltpu

D_MODEL = 2048
GRID_W = 64
ATT_HEADS = 8
ATT_KV_HEADS = 2
ATT_GROUP = ATT_HEADS // ATT_KV_HEADS
HEAD_DIM = 128
ATT_WIDTH = ATT_HEADS * HEAD_DIM
KV_WIDTH = ATT_KV_HEADS * HEAD_DIM
ROPE_THETA = 10000.0
HG_WIDTH = 1024
HG_HEADS = 8
HG_HEAD_V = 128
HG_EXPAND = 128
IN_COLS = ATT_WIDTH + 2 * KV_WIDTH + 5 * HG_WIDTH
D_FF = 5632
EPS = 1e-6
LOG2E = math.log2(math.e)

LANES = 128
BF16_SUBLANES = 16
VMEM_LIMIT = 56 * 1024 * 1024
FFN_VMEM_LIMIT = 60 * 1024 * 1024

COL_TILE = 512
N_COL_TILES = IN_COLS // COL_TILE
HG_CHUNK = 64
HG_LEVELS = (32, 16, 8, 4, 2, 1)
HG_HALF = 512
ATT_ROW_BLOCK = 128
NORM_ROW_BLOCK = 64
FFN_COL_GROUP = 512


def _params(*sem):
    return pltpu.CompilerParams(dimension_semantics=sem, vmem_limit_bytes=VMEM_LIMIT)


def _rms(x, w):
    return x * lax.rsqrt(jnp.mean(x * x, axis=-1, keepdims=True) + EPS) * w


def _sigmoid(x):
    return 1.0 / (1.0 + jnp.exp(-x))


def _silu(x):
    return x * _sigmoid(x)


def _mod_kernel(c_ref, w_ref, b_ref, o_ref):
    s = _silu(c_ref[...])
    o_ref[...] = jnp.sum(s * w_ref[...], axis=0, keepdims=True) + b_ref[...]


def _modulation(c_col, w_ada, b_ada):
    d, n = w_ada.shape
    tn = 1024
    return pl.pallas_call(
        _mod_kernel,
        grid=(n // tn,),
        in_specs=[pl.BlockSpec((d, 1), lambda j: (0, 0)),
                  pl.BlockSpec((d, tn), lambda j: (0, j)),
                  pl.BlockSpec((1, tn), lambda j: (0, j))],
        out_specs=pl.BlockSpec((1, tn), lambda j: (0, j)),
        out_shape=jax.ShapeDtypeStruct((1, n), jnp.float32),
        compiler_params=_params("arbitrary"),
        name="adaln_mod",
    )(c_col, w_ada, b_ada)


def _in_proj_kernel(x_ref, nw_ref, sc_ref, sh_ref, w_ref, o_ref, h_scr):
    @pl.when(pl.program_id(1) == 0)
    def _():
        for r0 in range(0, x_ref.shape[0], NORM_ROW_BLOCK):
            rows = slice(r0, r0 + NORM_ROW_BLOCK)
            h = _rms(x_ref[rows, :], nw_ref[...]) * (1.0 + sc_ref[...]) + sh_ref[...]
            h_scr[rows, :] = h.astype(jnp.bfloat16)

    o_ref[0] = jnp.dot(h_scr[...], w_ref[...].astype(jnp.bfloat16),
                       preferred_element_type=jnp.float32)


def _in_proj(x2, nw, sc, sh, w_in, tm):
    s, d = x2.shape
    vec = pl.BlockSpec((1, d), lambda i, j: (0, 0))
    return pl.pallas_call(
        _in_proj_kernel,
        grid=(s // tm, N_COL_TILES),
        in_specs=[pl.BlockSpec((tm, d), lambda i, j: (i, 0)), vec, vec, vec,
                  pl.BlockSpec((d, COL_TILE), lambda i, j: (0, j))],
        out_specs=pl.BlockSpec((1, tm, COL_TILE), lambda i, j: (j, i, 0)),
        out_shape=jax.ShapeDtypeStruct((N_COL_TILES, s, COL_TILE), jnp.float32),
        scratch_shapes=[pltpu.VMEM((tm, d), jnp.bfloat16)],
        compiler_params=_params("parallel", "arbitrary"),
        name="in_proj",
    )(x2, nw, sc, sh, w_in)


def _rope(y, cos, sin_a, sin_b):
    return (y * cos + pltpu.roll(y, LANES - 32, axis=1) * sin_a
            + pltpu.roll(y, 32, axis=1) * sin_b)


def _att_prep_kernel(q0_ref, q1_ref, kv_ref, cos_ref, sa_ref, sb_ref, qn_ref, kn_ref,
                     q_out, k_out, v_out):
    cos, sa, sb = cos_ref[...], sa_ref[...], sb_ref[...]
    scale = HEAD_DIM ** -0.5 * LOG2E
    for t, src in enumerate((q0_ref, q1_ref)):
        for hh in range(COL_TILE // HEAD_DIM):
            xh = src[0, :, hh * HEAD_DIM:(hh + 1) * HEAD_DIM]
            y = _rope(_rms(xh, qn_ref[...]), cos, sa, sb) * scale
            col = t * COL_TILE + hh * HEAD_DIM
            q_out[:, col:col + HEAD_DIM] = y.astype(jnp.bfloat16)
    for hh in range(ATT_KV_HEADS):
        xh = kv_ref[0, :, hh * HEAD_DIM:(hh + 1) * HEAD_DIM]
        y = _rope(_rms(xh, kn_ref[...]), cos, sa, sb)
        k_out[:, hh * HEAD_DIM:(hh + 1) * HEAD_DIM] = y.astype(jnp.bfloat16)
    ones = jnp.ones((v_out.shape[0], HEAD_DIM), jnp.bfloat16)
    for hh in range(ATT_KV_HEADS):
        vh = kv_ref[0, :, KV_WIDTH + hh * HEAD_DIM:KV_WIDTH + (hh + 1) * HEAD_DIM]
        v_out[:, 2 * hh * HEAD_DIM:(2 * hh + 1) * HEAD_DIM] = vh.astype(jnp.bfloat16)
        v_out[:, (2 * hh + 1) * HEAD_DIM:(2 * hh + 2) * HEAD_DIM] = ones


def _att_prep(proj, cos, sin_a, sin_b, qn, kn, tm):
    s = proj.shape[1]
    tile = lambda t: pl.BlockSpec((1, tm, COL_TILE), lambda i, t=t: (t, i, 0))
    tab = pl.BlockSpec((tm, HEAD_DIM), lambda i: (i, 0))
    vec = pl.BlockSpec((1, HEAD_DIM), lambda i: (0, 0))
    return pl.pallas_call(
        _att_prep_kernel,
        grid=(s // tm,),
        in_specs=[tile(0), tile(1), tile(2), tab, tab, tab, vec, vec],
        out_specs=[pl.BlockSpec((tm, ATT_WIDTH), lambda i: (i, 0)),
                   pl.BlockSpec((tm, KV_WIDTH), lambda i: (i, 0)),
                   pl.BlockSpec((tm, 2 * KV_WIDTH), lambda i: (i, 0))],
        out_shape=[jax.ShapeDtypeStruct((s, ATT_WIDTH), jnp.bfloat16),
                   jax.ShapeDtypeStruct((s, KV_WIDTH), jnp.bfloat16),
                   jax.ShapeDtypeStruct((s, 2 * KV_WIDTH), jnp.bfloat16)],
        compiler_params=_params("parallel"),
        name="att_prep",
    )(proj, proj, proj, cos, sin_a, sin_b, qn, kn)


def _attention_kernel(q_ref, k_ref, v_ref, o_ref, m_scr, acc_scr, alpha_scr, s_scr, p_scr, *,
                      tq, tk):
    n_kv = k_ref.shape[0] // tk
    m_rows = ATT_GROUP * tq
    q = jnp.concatenate([q_ref[:, g * HEAD_DIM:(g + 1) * HEAD_DIM] for g in range(ATT_GROUP)],
                        axis=0)
    m_scr[...] = jnp.full(m_scr.shape, -jnp.inf, jnp.float32)
    acc_scr[...] = jnp.zeros(acc_scr.shape, jnp.float32)

    def scores(j):
        off = pl.multiple_of(j * tk, tk)
        return lax.dot_general(q, k_ref[pl.ds(off, tk), :], (((1,), (1,)), ((), ())),
                               preferred_element_type=jnp.float32)

    def softmax_pv(slot, j):
        for r0 in range(0, m_rows, ATT_ROW_BLOCK):
            rows = slice(r0, r0 + ATT_ROW_BLOCK)
            s = s_scr[slot, rows, :]
            m_old = m_scr[rows, :]
            m_new = jnp.maximum(m_old, jnp.max(s, axis=-1, keepdims=True))
            alpha_scr[rows, :] = jnp.exp2(m_old - m_new)
            m_scr[rows, :] = m_new
            p_scr[rows, :] = jnp.concatenate(
                [jnp.exp2(s[:, t * LANES:(t + 1) * LANES] - m_new) for t in range(tk // LANES)],
                axis=-1).astype(jnp.bfloat16)
        off = pl.multiple_of(j * tk, tk)
        pv = jnp.dot(p_scr[...], v_ref[pl.ds(off, tk), :], preferred_element_type=jnp.float32)
        alpha = alpha_scr[...]
        acc_scr[...] = jnp.concatenate([alpha, alpha], axis=-1) * acc_scr[...] + pv

    s_scr[0] = scores(0)

    def body(jj, carry):
        j = 2 * jj
        s_scr[1] = scores(j + 1)
        softmax_pv(0, j)
        s_scr[0] = scores(jnp.minimum(j + 2, n_kv - 1))
        softmax_pv(1, j + 1)
        return carry

    lax.fori_loop(0, n_kv // 2, body, 0)
    out = acc_scr[:, 0:HEAD_DIM] / acc_scr[:, HEAD_DIM:2 * HEAD_DIM]
    for g in range(ATT_GROUP):
        o_ref[:, g * HEAD_DIM:(g + 1) * HEAD_DIM] = out[g * tq:(g + 1) * tq].astype(o_ref.dtype)


def _attention(qt, kt, vt, tq, tk):
    s = qt.shape[0]
    gw = ATT_GROUP * HEAD_DIM
    return pl.pallas_call(
        functools.partial(_attention_kernel, tq=tq, tk=tk),
        grid=(ATT_KV_HEADS, s // tq),
        in_specs=[pl.BlockSpec((tq, gw), lambda h, i: (i, h)),
                  pl.BlockSpec((s, HEAD_DIM), lambda h, i: (0, h)),
                  pl.BlockSpec((s, 2 * HEAD_DIM), lambda h, i: (0, h))],
        out_specs=pl.BlockSpec((tq, gw), lambda h, i: (i, h)),
        out_shape=jax.ShapeDtypeStruct((s, ATT_WIDTH), jnp.bfloat16),
        scratch_shapes=[pltpu.VMEM((ATT_GROUP * tq, LANES), jnp.float32),
                        pltpu.VMEM((ATT_GROUP * tq, 2 * HEAD_DIM), jnp.float32),
                        pltpu.VMEM((ATT_GROUP * tq, LANES), jnp.float32),
                        pltpu.VMEM((2, ATT_GROUP * tq, tk), jnp.float32),
                        pltpu.VMEM((ATT_GROUP * tq, tk), jnp.bfloat16)],
        compiler_params=_params("parallel", "arbitrary"),
        name="attention",
    )(qt, kt, vt)


def _hgrn_tables(reverse):
    c = HG_CHUNK
    idx = np.arange(c)
    row, u = idx[:, None], idx[None, :]
    blocks = [(u >= row) if reverse else (u <= row), (u < row) if reverse else (u > row)]
    w = np.concatenate(blocks, axis=0).astype(np.float32)
    return jnp.asarray(np.tile(w, (1, 3)), jnp.bfloat16)


def _hgrn_level_masks(reverse):
    c = HG_CHUNK
    row = lax.broadcasted_iota(jnp.int32, (c, c), 0)
    col = lax.broadcasted_iota(jnp.int32, (c, c), 1)
    is_query, pair_mask = [], []
    for h in HG_LEVELS:
        q_row = (row % (2 * h) < h) if reverse else (row % (2 * h) >= h)
        q_col = (col % (2 * h) < h) if reverse else (col % (2 * h) >= h)
        same = (row // (2 * h)) == (col // (2 * h))
        pair_mask.append((same & q_row & jnp.logical_not(q_col)).astype(jnp.float32))
        is_query.append(q_row[:, 0:1])
    return is_query, pair_mask


def _hgrn_chunk(qf, kk, g, v, w3, is_query, pair_mask, state_ref, reverse):
    c = HG_CHUNK
    nt = (((1,), (1,)), ((), ()))
    g1 = g.astype(jnp.bfloat16)
    r1 = g - g1.astype(jnp.float32)
    g2 = r1.astype(jnp.bfloat16)
    g3 = (r1 - g2.astype(jnp.float32)).astype(jnp.bfloat16)
    sums = jnp.dot(w3, jnp.concatenate([g1, g2, g3], axis=0),
                   preferred_element_type=jnp.float32)
    b = sums[0:c]
    d_in = jnp.exp2(b)
    edge = d_in[0:1, :] if reverse else d_in[c - 1:c, :]
    q_in = (qf * d_in).astype(jnp.bfloat16)
    k_out = (kk * jnp.exp2(sums[c:2 * c])).astype(jnp.bfloat16)
    g_next = pltpu.roll(g, c - 1, axis=0)
    g_prev = pltpu.roll(g, 1, axis=0)
    pos4 = lax.broadcasted_iota(jnp.int32, (c, 1), 0) % 4
    sign = [jnp.where(q, 1.0, -1.0) for q in is_query]
    xs = []
    for l, h in enumerate(HG_LEVELS):
        if h >= 4:
            ref = h if reverse else h - 1
            b3 = b.reshape(c // (2 * h), 2 * h, b.shape[-1])
            bref = jnp.broadcast_to(b3[:, ref:ref + 1, :], b3.shape).reshape(b.shape)
            e_l = (b - bref) * sign[l]
        elif h == 2:
            zero = jnp.zeros_like(g)
            if reverse:
                e_l = jnp.where(pos4 == 0, g + g_next, jnp.where(pos4 == 1, g,
                                                                  jnp.where(pos4 == 2, zero, g_prev)))
            else:
                e_l = jnp.where(pos4 == 0, g_next, jnp.where(pos4 == 1, zero,
                                                             jnp.where(pos4 == 2, g, g + g_prev)))
        else:
            e_l = jnp.where(is_query[l], g, jnp.zeros_like(g))
        xs.append((jnp.where(is_query[l], qf, kk) * jnp.exp2(e_l)).astype(jnp.bfloat16))
    qk = qf * kk
    vb = v.astype(jnp.bfloat16)

    outs = []
    for h in range(HG_HALF // HG_EXPAND):
        sl = slice(h * HG_EXPAND, (h + 1) * HG_EXPAND)
        a = None
        for l in range(len(HG_LEVELS)):
            x = xs[l][:, sl]
            s_l = pair_mask[l] * lax.dot_general(x, x, nt, preferred_element_type=jnp.float32)
            a = s_l if a is None else a + s_l
        st = state_ref[h]
        oh = (jnp.dot(a.astype(jnp.bfloat16), vb[:, sl], preferred_element_type=jnp.float32)
              + lax.dot_general(q_in[:, sl], st.astype(jnp.bfloat16), nt,
                                preferred_element_type=jnp.float32)
              + jnp.sum(qk[:, sl], axis=-1, keepdims=True) * v[:, sl])
        outs.append(oh)
        upd = lax.dot_general(vb[:, sl], k_out[:, sl], (((0,), (0,)), ((), ())),
                              preferred_element_type=jnp.float32)
        state_ref[h] = st * edge[:, sl] + upd
    return jnp.concatenate(outs, axis=-1)


def _hgrn_kernel(qf_ref, qb_ref, ff_ref, fb_ref, if_ref, ib_ref, lbf_ref, lbb_ref, w3f_ref, w3b_ref,
                 of_ref, ob_ref, state_scr, *, rows):
    @pl.when(pl.program_id(1) == 0)
    def _():
        state_scr[...] = jnp.zeros(state_scr.shape, jnp.float32)

    def lower_bound(lb_ref):
        lbp = lb_ref[...]
        e = jnp.exp(lbp - jnp.max(lbp, axis=0, keepdims=True))
        return e[0:1, :] / jnp.sum(e, axis=0, keepdims=True)

    dirs = [(False, qf_ref, ff_ref, if_ref, of_ref, lower_bound(lbf_ref), w3f_ref[...]),
            (True, qb_ref, fb_ref, ib_ref, ob_ref, lower_bound(lbb_ref), w3b_ref[...])]
    masks = [_hgrn_level_masks(False), _hgrn_level_masks(True)]
    n_chunks = rows // HG_CHUNK
    for step in range(n_chunks):
        for d, (reverse, q_ref, f_ref, i_ref, o_ref, lb, w3) in enumerate(dirs):
            ci = n_chunks - 1 - step if reverse else step
            r = slice(ci * HG_CHUNK, (ci + 1) * HG_CHUNK)
            qf = _silu(q_ref[0, r, :]) * (HG_EXPAND ** -0.5)
            f = lb + (1.0 - lb) * _sigmoid(f_ref[0, r, :])
            o_ref[r, :] = _hgrn_chunk(qf, 1.0 - f, jnp.log2(f), i_ref[0, r, :], w3, masks[d][0],
                                      masks[d][1], state_scr.at[d], reverse)


def _hgrn(proj, lb_fwd, lb_bwd, rows):
    s = proj.shape[1]
    nblk = s // rows
    n_half = HG_WIDTH // HG_HALF
    w3f, w3b = _hgrn_tables(False), _hgrn_tables(True)
    fwd = lambda n: n
    bwd = lambda n: nblk - 1 - n

    def tile(t0, blk):
        return pl.BlockSpec((1, rows, COL_TILE), lambda h, n: (t0 + h, blk(n), 0))

    lb_spec = pl.BlockSpec((lb_fwd.shape[0], HG_HALF), lambda h, n: (0, h))
    w3_spec = pl.BlockSpec(w3f.shape, lambda h, n: (0, 0))
    out = jax.ShapeDtypeStruct((s, HG_WIDTH), jnp.float32)
    return pl.pallas_call(
        functools.partial(_hgrn_kernel, rows=rows),
        grid=(n_half, nblk),
        in_specs=[tile(3, fwd), tile(3, bwd), tile(5, fwd), tile(7, bwd), tile(9, fwd), tile(9, bwd),
                  lb_spec, lb_spec, w3_spec, w3_spec],
        out_specs=[pl.BlockSpec((rows, HG_HALF), lambda h, n: (fwd(n), h)),
                   pl.BlockSpec((rows, HG_HALF), lambda h, n: (bwd(n), h))],
        out_shape=[out, out],
        scratch_shapes=[pltpu.VMEM((2, HG_HALF // HG_EXPAND, HG_HEAD_V, HG_EXPAND), jnp.float32)],
        compiler_params=_params("parallel", "arbitrary"),
        name="hgrn",
    )(proj, proj, proj, proj, proj, proj, lb_fwd, lb_bwd, w3f, w3b)


def _mix_out_kernel(att_ref, of_ref, ob_ref, g0_ref, g1_ref, hn_ref, w_ref, x_ref, gate_ref,
                    npost_ref, npre_ref, sc_ref, sh_ref, x1_ref, h2_ref):
    hsum = of_ref[...] + ob_ref[...]
    gate = jnp.concatenate([g0_ref[0], g1_ref[0]], axis=-1)
    heads = []
    for h in range(HG_HEADS):
        sl = slice(h * HG_HEAD_V, (h + 1) * HG_HEAD_V)
        heads.append(_rms(hsum[:, sl], hn_ref[:, sl]) * _silu(gate[:, sl]))
    hg = jnp.concatenate(heads, axis=-1).astype(jnp.bfloat16)
    mix = (jnp.dot(att_ref[...], w_ref[0:ATT_WIDTH, :], preferred_element_type=jnp.float32)
           + jnp.dot(hg, w_ref[ATT_WIDTH:, :], preferred_element_type=jnp.float32))
    x1 = x_ref[...] + gate_ref[...] * _rms(mix, npost_ref[...])
    x1_ref[...] = x1
    h2 = _rms(x1, npre_ref[...]) * (1.0 + sc_ref[...]) + sh_ref[...]
    h2_ref[...] = h2.astype(jnp.bfloat16)


def _mix_out(o_att, o_f, o_b, proj, hn, w_out_bf16, x2, g1, npost, npre, sc2, sh2, tm):
    s, d = x2.shape
    vec = lambda n: pl.BlockSpec((1, n), lambda i: (0, 0))
    row = lambda n: pl.BlockSpec((tm, n), lambda i: (i, 0))
    gtile = lambda t: pl.BlockSpec((1, tm, COL_TILE), lambda i, t=t: (t, i, 0))
    return pl.pallas_call(
        _mix_out_kernel,
        grid=(s // tm,),
        in_specs=[row(ATT_WIDTH), row(HG_WIDTH), row(HG_WIDTH), gtile(11), gtile(12),
                  vec(HG_WIDTH),
                  pl.BlockSpec((d, d), lambda i: (0, 0), pipeline_mode=pl.Buffered(1)), row(d),
                  vec(d), vec(d), vec(d), vec(d), vec(d)],
        out_specs=[row(d), row(d)],
        out_shape=[jax.ShapeDtypeStruct((s, d), jnp.float32),
                   jax.ShapeDtypeStruct((s, d), jnp.bfloat16)],
        compiler_params=_params("parallel"),
        name="mix_out",
    )(o_att, o_f, o_b, proj, proj, hn, w_out_bf16, x2, g1, npost, npre, sc2, sh2)


def _gelu_tanh(x):
    return 0.5 * x * (1.0 + jnp.tanh(math.sqrt(2.0 / math.pi) * (x + 0.044715 * (x * x * x))))


def _ffn_kernel(h_ref, hp_ref, hn_ref, wa_ref, wb_ref, cwa_ref, cwb_ref, cba_ref, cbb_ref,
                wd_ref, x1_ref, gate_ref, npost_ref, o_ref, hext_scr, *, tm, tf):
    i, j = pl.program_id(0), pl.program_id(1)
    halo = BF16_SUBLANES
    n_ext = tm + 2 * halo

    @pl.when(j == 0)
    def _():
        hext_scr[0:halo, :] = jnp.where(i == 0, jnp.zeros_like(hp_ref[...]), hp_ref[...])
        hext_scr[halo:halo + tm, :] = h_ref[...]
        hext_scr[halo + tm:, :] = jnp.where(i == pl.num_programs(0) - 1,
                                            jnp.zeros_like(hn_ref[...]), hn_ref[...])
        o_ref[...] = jnp.zeros(o_ref.shape, o_ref.dtype)

    h_ext = hext_scr[...]

    def conv(u, cw, cb):
        u_prev = pltpu.roll(u, 1, axis=0)[halo:halo + tm]
        u_next = pltpu.roll(u, n_ext - 1, axis=0)[halo:halo + tm]
        return cw[0:1, :] * u_prev + cw[1:2, :] * u[halo:halo + tm] + cw[2:3, :] * u_next + cb

    groups = [slice(c0, c0 + FFN_COL_GROUP) for c0 in range(0, tf, FFN_COL_GROUP)]
    ups = [(jnp.dot(h_ext, wa_ref[:, cols], preferred_element_type=jnp.float32),
            jnp.dot(h_ext, wb_ref[:, cols], preferred_element_type=jnp.float32)) for cols in groups]
    for cols, (ua, ub) in zip(groups, ups):
        a = conv(ua, cwa_ref[:, cols], cba_ref[:, cols])
        b = conv(ub, cwb_ref[:, cols], cbb_ref[:, cols])
        act = (_gelu_tanh(a) * b).astype(jnp.bfloat16)
        o_ref[...] += jnp.dot(act, wd_ref[cols, :], preferred_element_type=jnp.float32)

    @pl.when(j == pl.num_programs(1) - 1)
    def _():
        for r0 in range(0, tm, NORM_ROW_BLOCK):
            rows = slice(r0, r0 + NORM_ROW_BLOCK)
            o_ref[rows, :] = x1_ref[rows, :] + gate_ref[...] * _rms(o_ref[rows, :], npost_ref[...])


def _ffn(h2, w_up_bf16, conv_w, conv_b, w_down_bf16, x1, g2, npost, tm, tf):
    s, d = x1.shape
    nf = D_FF // tf
    hb = tm // BF16_SUBLANES
    n_hblk = s // BF16_SUBLANES
    vec = pl.BlockSpec((1, d), lambda i, j: (0, 0))
    single = pl.Buffered(1)
    return pl.pallas_call(
        functools.partial(_ffn_kernel, tm=tm, tf=tf),
        grid=(s // tm, nf),
        in_specs=[pl.BlockSpec((tm, d), lambda i, j: (i, 0), pipeline_mode=single),
                  pl.BlockSpec((BF16_SUBLANES, d), lambda i, j: (jnp.maximum(i * hb - 1, 0), 0)),
                  pl.BlockSpec((BF16_SUBLANES, d),
                               lambda i, j: (jnp.minimum((i + 1) * hb, n_hblk - 1), 0)),
                  pl.BlockSpec((d, tf), lambda i, j: (0, j)),
                  pl.BlockSpec((d, tf), lambda i, j: (0, j + nf)),
                  pl.BlockSpec((3, tf), lambda i, j: (0, j)),
                  pl.BlockSpec((3, tf), lambda i, j: (0, j + nf)),
                  pl.BlockSpec((1, tf), lambda i, j: (0, j)),
                  pl.BlockSpec((1, tf), lambda i, j: (0, j + nf)),
                  pl.BlockSpec((tf, d), lambda i, j: (j, 0)),
                  pl.BlockSpec((tm, d), lambda i, j: (i, 0), pipeline_mode=single), vec, vec],
        out_specs=pl.BlockSpec((tm, d), lambda i, j: (i, 0)),
        out_shape=jax.ShapeDtypeStruct((s, d), jnp.float32),
        scratch_shapes=[pltpu.VMEM((tm + 2 * BF16_SUBLANES, d), jnp.bfloat16)],
        compiler_params=pltpu.CompilerParams(dimension_semantics=("parallel", "arbitrary"),
                                             vmem_limit_bytes=FFN_VMEM_LIMIT),
        name="conv_ffn",
    )(h2, h2, h2, w_up_bf16, w_up_bf16, conv_w, conv_w, conv_b, conv_b, w_down_bf16, x1, g2, npost)


def _rope_tables(s):
    rows = s // GRID_W
    t = jnp.arange(s)
    r = (t // GRID_W - rows // 2).astype(jnp.float32)
    cpos = (t % GRID_W - GRID_W // 2).astype(jnp.float32)
    axis_dim = HEAD_DIM // 2
    inv = ROPE_THETA ** (-(2.0 * jnp.arange(axis_dim // 2, dtype=jnp.float32)) / axis_dim)
    ang_r = r[:, None] * inv[None, :]
    ang_c = cpos[:, None] * inv[None, :]
    ang = jnp.concatenate([ang_r, ang_r, ang_c, ang_c], axis=-1)
    cos, sin = jnp.cos(ang), jnp.sin(ang)
    low = (jnp.arange(HEAD_DIM) % axis_dim) < (axis_dim // 2)
    return cos, jnp.where(low, -sin, 0.0), jnp.where(low, 0.0, sin)


def kernel(x, c, w_ada, b_ada, norm_mix_pre, norm_mix_post, w_in, q_norm, k_norm, hg_lower_bound,
           hg_out_norm, w_out, norm_ffn_pre, norm_ffn_post, w_up, conv_w, conv_b, w_down):
    batch, s, d = x.shape
    assert batch == 1 and d == D_MODEL and s % GRID_W == 0
    layer = 0
    x2 = x.reshape(s, d)
    tm_big = min(1024, s)
    tm_mid = min(512, s)
    tm_small = min(512, s)

    mod = _modulation(c.reshape(d, 1), w_ada[layer], b_ada[layer].reshape(1, -1))
    sh1, sc1, g1, sh2, sc2, g2 = [mod[:, k * d:(k + 1) * d] for k in range(6)]

    proj = _in_proj(x2, norm_mix_pre[layer].reshape(1, d), sc1, sh1,
                    w_in[layer], tm_big)

    cos, sin_a, sin_b = _rope_tables(s)
    qt, kt, vt = _att_prep(proj, cos, sin_a, sin_b, q_norm[layer].reshape(1, -1),
                           k_norm[layer].reshape(1, -1), tm_mid)
    o_att = _attention(qt, kt, vt, tq=min(512, s), tk=min(512, s))

    rows = min(512, s)
    o_f, o_b = _hgrn(proj, hg_lower_bound[:, 0, :], hg_lower_bound[:, 1, :], rows)

    x1, h2 = _mix_out(o_att, o_f, o_b, proj, hg_out_norm[layer].reshape(1, -1),
                      w_out[layer].astype(jnp.bfloat16), x2, g1,
                      norm_mix_post[layer].reshape(1, d), norm_ffn_pre[layer].reshape(1, d),
                      sc2, sh2, tm_small)

    out = _ffn(h2, w_up[layer].astype(jnp.bfloat16), conv_w[layer], conv_b[layer].reshape(1, -1),
               w_down[layer].astype(jnp.bfloat16), x1, g2, norm_ffn_post[layer].reshape(1, d),
               tm_big, 512)
    return out.reshape(batch, s, d)
```

```python
import functools
import math

import numpy as np
import jax
import jax.numpy as jnp
from jax import lax
from jax.experimental import pallas as pl
from jax.experimental.pallas import tpu as pltpu

D_MODEL = 2048
GRID_W = 64
ATT_HEADS = 8
ATT_KV_HEADS = 2
ATT_GROUP = ATT_HEADS // ATT_KV_HEADS
HEAD_DIM = 128
ATT_WIDTH = ATT_HEADS * HEAD_DIM
KV_WIDTH = ATT_KV_HEADS * HEAD_DIM
ROPE_THETA = 10000.0
HG_WIDTH = 1024
HG_HEADS = 8
HG_HEAD_V = 128
HG_EXPAND = 128
IN_COLS = ATT_WIDTH + 2 * KV_WIDTH + 5 * HG_WIDTH
D_FF = 5632
EPS = 1e-6
LOG2E = math.log2(math.e)

LANES = 128
BF16_SUBLANES = 16
VMEM_LIMIT = 56 * 1024 * 1024
FFN_VMEM_LIMIT = 60 * 1024 * 1024

COL_TILE = 512
N_COL_TILES = IN_COLS // COL_TILE
HG_CHUNK = 128
HG_LEVELS = (64, 32, 16, 8, 4, 2, 1)
HG_HALF = 512
ATT_ROW_BLOCK = 128
NORM_ROW_BLOCK = 64
FFN_COL_GROUP = 512


def _params(*sem):
    return pltpu.CompilerParams(dimension_semantics=sem, vmem_limit_bytes=VMEM_LIMIT)


def _rms(x, w):
    return x * lax.rsqrt(jnp.mean(x * x, axis=-1, keepdims=True) + EPS) * w


def _sigmoid(x):
    return 1.0 / (1.0 + jnp.exp(-x))


def _silu(x):
    return x * _sigmoid(x)


def _mod_kernel(c_ref, w_ref, b_ref, o_ref):
    s = _silu(c_ref[...])
    o_ref[...] = jnp.sum(s * w_ref[...], axis=0, keepdims=True) + b_ref[...]


def _modulation(c_col, w_ada, b_ada):
    d, n = w_ada.shape
    tn = 1024
    return pl.pallas_call(
        _mod_kernel,
        grid=(n // tn,),
        in_specs=[pl.BlockSpec((d, 1), lambda j: (0, 0)),
                  pl.BlockSpec((d, tn), lambda j: (0, j)),
                  pl.BlockSpec((1, tn), lambda j: (0, j))],
        out_specs=pl.BlockSpec((1, tn), lambda j: (0, j)),
        out_shape=jax.ShapeDtypeStruct((1, n), jnp.float32),
        compiler_params=_params("arbitrary"),
        name="adaln_mod",
    )(c_col, w_ada, b_ada)


def _in_proj_kernel(x_ref, nw_ref, sc_ref, sh_ref, w_ref, o_ref, h_scr):
    @pl.when(pl.program_id(1) == 0)
    def _():
        for r0 in range(0, x_ref.shape[0], NORM_ROW_BLOCK):
            rows = slice(r0, r0 + NORM_ROW_BLOCK)
            h = _rms(x_ref[rows, :], nw_ref[...]) * (1.0 + sc_ref[...]) + sh_ref[...]
            h_scr[rows, :] = h.astype(jnp.bfloat16)

    o_ref[0] = jnp.dot(h_scr[...], w_ref[...], preferred_element_type=jnp.float32)


def _in_proj(x2, nw, sc, sh, w_in, tm):
    s, d = x2.shape
    vec = pl.BlockSpec((1, d), lambda i, j: (0, 0))
    return pl.pallas_call(
        _in_proj_kernel,
        grid=(s // tm, N_COL_TILES),
        in_specs=[pl.BlockSpec((tm, d), lambda i, j: (i, 0)), vec, vec, vec,
                  pl.BlockSpec((d, COL_TILE), lambda i, j: (0, j))],
        out_specs=pl.BlockSpec((1, tm, COL_TILE), lambda i, j: (j, i, 0)),
        out_shape=jax.ShapeDtypeStruct((N_COL_TILES, s, COL_TILE), jnp.float32),
        scratch_shapes=[pltpu.VMEM((tm, d), jnp.bfloat16)],
        compiler_params=_params("parallel", "arbitrary"),
        name="in_proj",
    )(x2, nw, sc, sh, w_in)


def _rope(y, cos, sin_a, sin_b):
    return (y * cos + pltpu.roll(y, LANES - 32, axis=1) * sin_a
            + pltpu.roll(y, 32, axis=1) * sin_b)


def _att_prep_kernel(q0_ref, q1_ref, kv_ref, cr_ref, ar_ref, br_ref, cc_ref, ac_ref, bc_ref,
                     qn_ref, kn_ref, q_out, k_out, v_out):
    def table(row_ref, col_ref):
        n = row_ref.shape[0]
        rows = jnp.concatenate([jnp.broadcast_to(row_ref[k:k + 1, :], (GRID_W, HEAD_DIM))
                                for k in range(n)], axis=0)
        return rows + jnp.concatenate([col_ref[...]] * n, axis=0)

    cos, sa, sb = table(cr_ref, cc_ref), table(ar_ref, ac_ref), table(br_ref, bc_ref)
    scale = HEAD_DIM ** -0.5 * LOG2E
    for t, src in enumerate((q0_ref, q1_ref)):
        for hh in range(COL_TILE // HEAD_DIM):
            xh = src[0, :, hh * HEAD_DIM:(hh + 1) * HEAD_DIM]
            y = _rope(_rms(xh, qn_ref[...]), cos, sa, sb) * scale
            col = t * COL_TILE + hh * HEAD_DIM
            q_out[:, col:col + HEAD_DIM] = y.astype(jnp.bfloat16)
    for hh in range(ATT_KV_HEADS):
        xh = kv_ref[0, :, hh * HEAD_DIM:(hh + 1) * HEAD_DIM]
        y = _rope(_rms(xh, kn_ref[...]), cos, sa, sb)
        k_out[:, hh * HEAD_DIM:(hh + 1) * HEAD_DIM] = y.astype(jnp.bfloat16)
    ones = jnp.ones((v_out.shape[0], HEAD_DIM), jnp.bfloat16)
    for hh in range(ATT_KV_HEADS):
        vh = kv_ref[0, :, KV_WIDTH + hh * HEAD_DIM:KV_WIDTH + (hh + 1) * HEAD_DIM]
        v_out[:, 2 * hh * HEAD_DIM:(2 * hh + 1) * HEAD_DIM] = vh.astype(jnp.bfloat16)
        v_out[:, (2 * hh + 1) * HEAD_DIM:(2 * hh + 2) * HEAD_DIM] = ones


def _att_prep(proj, row_tabs, col_tabs, qn, kn, tm):
    s = proj.shape[1]
    tile = lambda t: pl.BlockSpec((1, tm, COL_TILE), lambda i, t=t: (t, i, 0))
    rtab = pl.BlockSpec((tm // GRID_W, HEAD_DIM), lambda i: (i, 0))
    ctab = pl.BlockSpec((GRID_W, HEAD_DIM), lambda i: (0, 0))
    vec = pl.BlockSpec((1, HEAD_DIM), lambda i: (0, 0))
    return pl.pallas_call(
        _att_prep_kernel,
        grid=(s // tm,),
        in_specs=[tile(0), tile(1), tile(2), rtab, rtab, rtab, ctab, ctab, ctab, vec, vec],
        out_specs=[pl.BlockSpec((tm, ATT_WIDTH), lambda i: (i, 0)),
                   pl.BlockSpec((tm, KV_WIDTH), lambda i: (i, 0)),
                   pl.BlockSpec((tm, 2 * KV_WIDTH), lambda i: (i, 0))],
        out_shape=[jax.ShapeDtypeStruct((s, ATT_WIDTH), jnp.bfloat16),
                   jax.ShapeDtypeStruct((s, KV_WIDTH), jnp.bfloat16),
                   jax.ShapeDtypeStruct((s, 2 * KV_WIDTH), jnp.bfloat16)],
        compiler_params=_params("parallel"),
        name="att_prep",
    )(proj, proj, proj, *row_tabs, *col_tabs, qn, kn)


def _attention_kernel(q_ref, k_ref, v_ref, o_ref, m_scr, acc_scr, alpha_scr, s_scr, p_scr, *,
                      tq, tk):
    n_kv = k_ref.shape[0] // tk
    m_rows = ATT_GROUP * tq
    q = jnp.concatenate([q_ref[:, g * HEAD_DIM:(g + 1) * HEAD_DIM] for g in range(ATT_GROUP)],
                        axis=0)
    m_scr[...] = jnp.full(m_scr.shape, -jnp.inf, jnp.float32)
    acc_scr[...] = jnp.zeros(acc_scr.shape, jnp.float32)

    def scores(j):
        off = pl.multiple_of(j * tk, tk)
        return lax.dot_general(q, k_ref[pl.ds(off, tk), :], (((1,), (1,)), ((), ())),
                               preferred_element_type=jnp.float32)

    def softmax_pv(slot, j):
        for r0 in range(0, m_rows, ATT_ROW_BLOCK):
            rows = slice(r0, r0 + ATT_ROW_BLOCK)
            s = s_scr[slot, rows, :]
            m_old = m_scr[rows, :]
            m_new = jnp.maximum(m_old, jnp.max(s, axis=-1, keepdims=True))
            alpha_scr[rows, :] = jnp.exp2(m_old - m_new)
            m_scr[rows, :] = m_new
            p_scr[rows, :] = jnp.concatenate(
                [jnp.exp2(s[:, t * LANES:(t + 1) * LANES] - m_new) for t in range(tk // LANES)],
                axis=-1).astype(jnp.bfloat16)
        off = pl.multiple_of(j * tk, tk)
        pv = jnp.dot(p_scr[...], v_ref[pl.ds(off, tk), :], preferred_element_type=jnp.float32)
        alpha = alpha_scr[...]
        acc_scr[...] = jnp.concatenate([alpha, alpha], axis=-1) * acc_scr[...] + pv

    s_scr[0] = scores(0)

    def body(jj, carry):
        j = 2 * jj
        s_scr[1] = scores(j + 1)
        softmax_pv(0, j)
        s_scr[0] = scores(jnp.minimum(j + 2, n_kv - 1))
        softmax_pv(1, j + 1)
        return carry

    lax.fori_loop(0, n_kv // 2, body, 0)
    out = acc_scr[:, 0:HEAD_DIM] / acc_scr[:, HEAD_DIM:2 * HEAD_DIM]
    for g in range(ATT_GROUP):
        o_ref[:, g * HEAD_DIM:(g + 1) * HEAD_DIM] = out[g * tq:(g + 1) * tq].astype(o_ref.dtype)


def _attention(qt, kt, vt, tq, tk):
    s = qt.shape[0]
    gw = ATT_GROUP * HEAD_DIM
    return pl.pallas_call(
        functools.partial(_attention_kernel, tq=tq, tk=tk),
        grid=(ATT_KV_HEADS, s // tq),
        in_specs=[pl.BlockSpec((tq, gw), lambda h, i: (i, h)),
                  pl.BlockSpec((s, HEAD_DIM), lambda h, i: (0, h)),
                  pl.BlockSpec((s, 2 * HEAD_DIM), lambda h, i: (0, h))],
        out_specs=pl.BlockSpec((tq, gw), lambda h, i: (i, h)),
        out_shape=jax.ShapeDtypeStruct((s, ATT_WIDTH), jnp.bfloat16),
        scratch_shapes=[pltpu.VMEM((ATT_GROUP * tq, LANES), jnp.float32),
                        pltpu.VMEM((ATT_GROUP * tq, 2 * HEAD_DIM), jnp.float32),
                        pltpu.VMEM((ATT_GROUP * tq, LANES), jnp.float32),
                        pltpu.VMEM((2, ATT_GROUP * tq, tk), jnp.float32),
                        pltpu.VMEM((ATT_GROUP * tq, tk), jnp.bfloat16)],
        compiler_params=_params("parallel", "arbitrary"),
        name="attention",
    )(qt, kt, vt)


def _hgrn_tables(reverse):
    c = HG_CHUNK
    idx = np.arange(c)
    row, u = idx[:, None], idx[None, :]
    blocks = [(u >= row) if reverse else (u <= row), (u < row) if reverse else (u > row)]
    w = np.concatenate(blocks, axis=0).astype(np.float32)
    return jnp.asarray(np.tile(w, (1, 3)), jnp.bfloat16)


def _hgrn_level_masks(reverse):
    c = HG_CHUNK
    row = lax.broadcasted_iota(jnp.int32, (c, c), 0)
    col = lax.broadcasted_iota(jnp.int32, (c, c), 1)
    is_query, pair_mask = [], []
    for h in HG_LEVELS:
        q_row = (row % (2 * h) < h) if reverse else (row % (2 * h) >= h)
        q_col = (col % (2 * h) < h) if reverse else (col % (2 * h) >= h)
        same = (row // (2 * h)) == (col // (2 * h))
        pair_mask.append((same & q_row & jnp.logical_not(q_col)).astype(jnp.float32))
        is_query.append(q_row[:, 0:1])
    return is_query, pair_mask


def _hgrn_chunk(qf, kk, g, v, w3, is_query, pair_mask, state_ref, reverse):
    c = HG_CHUNK
    nt = (((1,), (1,)), ((), ()))
    g1 = g.astype(jnp.bfloat16)
    r1 = g - g1.astype(jnp.float32)
    g2 = r1.astype(jnp.bfloat16)
    g3 = (r1 - g2.astype(jnp.float32)).astype(jnp.bfloat16)
    sums = jnp.dot(w3, jnp.concatenate([g1, g2, g3], axis=0),
                   preferred_element_type=jnp.float32)
    b = sums[0:c]
    d_in = jnp.exp2(b)
    edge = d_in[0:1, :] if reverse else d_in[c - 1:c, :]
    q_in = (qf * d_in).astype(jnp.bfloat16)
    k_out = (kk * jnp.exp2(sums[c:2 * c])).astype(jnp.bfloat16)
    g_next = pltpu.roll(g, c - 1, axis=0)
    g_prev = pltpu.roll(g, 1, axis=0)
    pos4 = lax.broadcasted_iota(jnp.int32, (c, 1), 0) % 4
    sign = [jnp.where(q, 1.0, -1.0) for q in is_query]
    xs = []
    for l, h in enumerate(HG_LEVELS):
        if h >= 4:
            ref = h if reverse else h - 1
            b3 = b.reshape(c // (2 * h), 2 * h, b.shape[-1])
            bref = jnp.broadcast_to(b3[:, ref:ref + 1, :], b3.shape).reshape(b.shape)
            e_l = (b - bref) * sign[l]
        elif h == 2:
            zero = jnp.zeros_like(g)
            if reverse:
                e_l = jnp.where(pos4 == 0, g + g_next, jnp.where(pos4 == 1, g,
                                                                  jnp.where(pos4 == 2, zero, g_prev)))
            else:
                e_l = jnp.where(pos4 == 0, g_next, jnp.where(pos4 == 1, zero,
                                                             jnp.where(pos4 == 2, g, g + g_prev)))
        else:
            e_l = jnp.where(is_query[l], g, jnp.zeros_like(g))
        xs.append((jnp.where(is_query[l], qf, kk) * jnp.exp2(e_l)).astype(jnp.bfloat16))
    qk = qf * kk
    vb = v.astype(jnp.bfloat16)

    outs = []
    for h in range(HG_HALF // HG_EXPAND):
        sl = slice(h * HG_EXPAND, (h + 1) * HG_EXPAND)
        a = None
        for l in range(len(HG_LEVELS)):
            x = xs[l][:, sl]
            s_l = pair_mask[l] * lax.dot_general(x, x, nt, preferred_element_type=jnp.float32)
            a = s_l if a is None else a + s_l
        st = state_ref[h]
        oh = (jnp.dot(a.astype(jnp.bfloat16), vb[:, sl], preferred_element_type=jnp.float32)
              + lax.dot_general(q_in[:, sl], st.astype(jnp.bfloat16), nt,
                                preferred_element_type=jnp.float32)
              + jnp.sum(qk[:, sl], axis=-1, keepdims=True) * v[:, sl])
        outs.append(oh)
        upd = lax.dot_general(vb[:, sl], k_out[:, sl], (((0,), (0,)), ((), ())),
                              preferred_element_type=jnp.float32)
        state_ref[h] = st * edge[:, sl] + upd
    return jnp.concatenate(outs, axis=-1)


def _hgrn_kernel(qf_ref, qb_ref, ff_ref, fb_ref, if_ref, ib_ref, lbf_ref, lbb_ref, w3f_ref, w3b_ref,
                 of_ref, ob_ref, state_scr, *, rows):
    @pl.when(pl.program_id(1) == 0)
    def _():
        state_scr[...] = jnp.zeros(state_scr.shape, jnp.float32)

    def lower_bound(lb_ref):
        lbp = lb_ref[...]
        e = jnp.exp(lbp - jnp.max(lbp, axis=0, keepdims=True))
        return e[0:1, :] / jnp.sum(e, axis=0, keepdims=True)

    dirs = [(False, qf_ref, ff_ref, if_ref, of_ref, lower_bound(lbf_ref), w3f_ref[...]),
            (True, qb_ref, fb_ref, ib_ref, ob_ref, lower_bound(lbb_ref), w3b_ref[...])]
    masks = [_hgrn_level_masks(False), _hgrn_level_masks(True)]
    n_chunks = rows // HG_CHUNK
    for step in range(n_chunks):
        for d, (reverse, q_ref, f_ref, i_ref, o_ref, lb, w3) in enumerate(dirs):
            ci = n_chunks - 1 - step if reverse else step
            r = slice(ci * HG_CHUNK, (ci + 1) * HG_CHUNK)
            qf = _silu(q_ref[0, r, :]) * (HG_EXPAND ** -0.5)
            f = lb + (1.0 - lb) * _sigmoid(f_ref[0, r, :])
            o_ref[r, :] = _hgrn_chunk(qf, 1.0 - f, jnp.log2(f), i_ref[0, r, :], w3, masks[d][0],
                                      masks[d][1], state_scr.at[d], reverse)


def _hgrn(proj, lb_fwd, lb_bwd, rows):
    s = proj.shape[1]
    nblk = s // rows
    n_half = HG_WIDTH // HG_HALF
    w3f, w3b = _hgrn_tables(False), _hgrn_tables(True)
    fwd = lambda n: n
    bwd = lambda n: nblk - 1 - n

    def tile(t0, blk):
        return pl.BlockSpec((1, rows, COL_TILE), lambda h, n: (t0 + h, blk(n), 0))

    lb_spec = pl.BlockSpec((lb_fwd.shape[0], HG_HALF), lambda h, n: (0, h))
    w3_spec = pl.BlockSpec(w3f.shape, lambda h, n: (0, 0))
    out = jax.ShapeDtypeStruct((s, HG_WIDTH), jnp.float32)
    return pl.pallas_call(
        functools.partial(_hgrn_kernel, rows=rows),
        grid=(n_half, nblk),
        in_specs=[tile(3, fwd), tile(3, bwd), tile(5, fwd), tile(7, bwd), tile(9, fwd), tile(9, bwd),
                  lb_spec, lb_spec, w3_spec, w3_spec],
        out_specs=[pl.BlockSpec((rows, HG_HALF), lambda h, n: (fwd(n), h)),
                   pl.BlockSpec((rows, HG_HALF), lambda h, n: (bwd(n), h))],
        out_shape=[out, out],
        scratch_shapes=[pltpu.VMEM((2, HG_HALF // HG_EXPAND, HG_HEAD_V, HG_EXPAND), jnp.float32)],
        compiler_params=_params("parallel", "arbitrary"),
        name="hgrn",
    )(proj, proj, proj, proj, proj, proj, lb_fwd, lb_bwd, w3f, w3b)


def _mix_out_kernel(att_ref, of_ref, ob_ref, g0_ref, g1_ref, hn_ref, w_ref, x_ref, gate_ref,
                    npost_ref, npre_ref, sc_ref, sh_ref, x1_ref, h2_ref):
    hsum = of_ref[...] + ob_ref[...]
    gate = jnp.concatenate([g0_ref[0], g1_ref[0]], axis=-1)
    heads = []
    for h in range(HG_HEADS):
        sl = slice(h * HG_HEAD_V, (h + 1) * HG_HEAD_V)
        heads.append(_rms(hsum[:, sl], hn_ref[:, sl]) * _silu(gate[:, sl]))
    hg = jnp.concatenate(heads, axis=-1).astype(jnp.bfloat16)
    mix = (jnp.dot(att_ref[...], w_ref[0:ATT_WIDTH, :], preferred_element_type=jnp.float32)
           + jnp.dot(hg, w_ref[ATT_WIDTH:, :], preferred_element_type=jnp.float32))
    x1 = x_ref[...] + gate_ref[...] * _rms(mix, npost_ref[...])
    x1_ref[...] = x1
    h2 = _rms(x1, npre_ref[...]) * (1.0 + sc_ref[...]) + sh_ref[...]
    h2_ref[...] = h2.astype(jnp.bfloat16)


def _mix_out(o_att, o_f, o_b, proj, hn, w_out_bf16, x2, g1, npost, npre, sc2, sh2, tm):
    s, d = x2.shape
    vec = lambda n: pl.BlockSpec((1, n), lambda i: (0, 0))
    row = lambda n: pl.BlockSpec((tm, n), lambda i: (i, 0))
    gtile = lambda t: pl.BlockSpec((1, tm, COL_TILE), lambda i, t=t: (t, i, 0))
    return pl.pallas_call(
        _mix_out_kernel,
        grid=(s // tm,),
        in_specs=[row(ATT_WIDTH), row(HG_WIDTH), row(HG_WIDTH), gtile(11), gtile(12),
                  vec(HG_WIDTH),
                  pl.BlockSpec((d, d), lambda i: (0, 0), pipeline_mode=pl.Buffered(1)), row(d),
                  vec(d), vec(d), vec(d), vec(d), vec(d)],
        out_specs=[row(d), row(d)],
        out_shape=[jax.ShapeDtypeStruct((s, d), jnp.float32),
                   jax.ShapeDtypeStruct((s, d), jnp.bfloat16)],
        compiler_params=_params("parallel"),
        name="mix_out",
    )(o_att, o_f, o_b, proj, proj, hn, w_out_bf16, x2, g1, npost, npre, sc2, sh2)


def _gelu_tanh(x):
    return 0.5 * x * (1.0 + jnp.tanh(math.sqrt(2.0 / math.pi) * (x + 0.044715 * (x * x * x))))


def _ffn_kernel(h_ref, hp_ref, hn_ref, wa_ref, wb_ref, cwa_ref, cwb_ref, cba_ref, cbb_ref,
                wd_ref, x1_ref, gate_ref, npost_ref, o_ref, hext_scr, *, tm, tf):
    i, j = pl.program_id(0), pl.program_id(1)
    halo = BF16_SUBLANES
    n_ext = tm + 2 * halo

    @pl.when(j == 0)
    def _():
        hext_scr[0:halo, :] = jnp.where(i == 0, jnp.zeros_like(hp_ref[...]), hp_ref[...])
        hext_scr[halo:halo + tm, :] = h_ref[...]
        hext_scr[halo + tm:, :] = jnp.where(i == pl.num_programs(0) - 1,
                                            jnp.zeros_like(hn_ref[...]), hn_ref[...])
        o_ref[...] = jnp.zeros(o_ref.shape, o_ref.dtype)

    h_ext = hext_scr[...]

    def conv(u, cw, cb):
        u_prev = pltpu.roll(u, 1, axis=0)[halo:halo + tm]
        u_next = pltpu.roll(u, n_ext - 1, axis=0)[halo:halo + tm]
        return cw[0:1, :] * u_prev + cw[1:2, :] * u[halo:halo + tm] + cw[2:3, :] * u_next + cb

    groups = [slice(c0, c0 + FFN_COL_GROUP) for c0 in range(0, tf, FFN_COL_GROUP)]
    ups = [(jnp.dot(h_ext, wa_ref[:, cols], preferred_element_type=jnp.float32),
            jnp.dot(h_ext, wb_ref[:, cols], preferred_element_type=jnp.float32)) for cols in groups]
    for cols, (ua, ub) in zip(groups, ups):
        a = conv(ua, cwa_ref[:, cols], cba_ref[:, cols])
        b = conv(ub, cwb_ref[:, cols], cbb_ref[:, cols])
        act = (_gelu_tanh(a) * b).astype(jnp.bfloat16)
        o_ref[...] += jnp.dot(act, wd_ref[cols, :], preferred_element_type=jnp.float32)

    @pl.when(j == pl.num_programs(1) - 1)
    def _():
        for r0 in range(0, tm, NORM_ROW_BLOCK):
            rows = slice(r0, r0 + NORM_ROW_BLOCK)
            o_ref[rows, :] = x1_ref[rows, :] + gate_ref[...] * _rms(o_ref[rows, :], npost_ref[...])


def _ffn(h2, w_up_bf16, conv_w, conv_b, w_down_bf16, x1, g2, npost, tm, tf):
    s, d = x1.shape
    nf = D_FF // tf
    hb = tm // BF16_SUBLANES
    n_hblk = s // BF16_SUBLANES
    vec = pl.BlockSpec((1, d), lambda i, j: (0, 0))
    single = pl.Buffered(1)
    return pl.pallas_call(
        functools.partial(_ffn_kernel, tm=tm, tf=tf),
        grid=(s // tm, nf),
        in_specs=[pl.BlockSpec((tm, d), lambda i, j: (i, 0), pipeline_mode=single),
                  pl.BlockSpec((BF16_SUBLANES, d), lambda i, j: (jnp.maximum(i * hb - 1, 0), 0)),
                  pl.BlockSpec((BF16_SUBLANES, d),
                               lambda i, j: (jnp.minimum((i + 1) * hb, n_hblk - 1), 0)),
                  pl.BlockSpec((d, tf), lambda i, j: (0, j)),
                  pl.BlockSpec((d, tf), lambda i, j: (0, j + nf)),
                  pl.BlockSpec((3, tf), lambda i, j: (0, j)),
                  pl.BlockSpec((3, tf), lambda i, j: (0, j + nf)),
                  pl.BlockSpec((1, tf), lambda i, j: (0, j)),
                  pl.BlockSpec((1, tf), lambda i, j: (0, j + nf)),
                  pl.BlockSpec((tf, d), lambda i, j: (j, 0)),
                  pl.BlockSpec((tm, d), lambda i, j: (i, 0), pipeline_mode=single), vec, vec],
        out_specs=pl.BlockSpec((tm, d), lambda i, j: (i, 0)),
        out_shape=jax.ShapeDtypeStruct((s, d), jnp.float32),
        scratch_shapes=[pltpu.VMEM((tm + 2 * BF16_SUBLANES, d), jnp.bfloat16)],
        compiler_params=pltpu.CompilerParams(dimension_semantics=("parallel", "arbitrary"),
                                             vmem_limit_bytes=FFN_VMEM_LIMIT),
        name="conv_ffn",
    )(h2, h2, h2, w_up_bf16, w_up_bf16, conv_w, conv_w, conv_b, conv_b, w_down_bf16, x1, g2, npost)


def _rope_tables(s):
    rows = s // GRID_W
    axis_dim = HEAD_DIM // 2
    inv = ROPE_THETA ** (-(2.0 * jnp.arange(axis_dim // 2, dtype=jnp.float32)) / axis_dim)
    r = (jnp.arange(rows) - rows // 2).astype(jnp.float32)
    cpos = (jnp.arange(GRID_W) - GRID_W // 2).astype(jnp.float32)
    low = (jnp.arange(axis_dim) % axis_dim) < (axis_dim // 2)

    def parts(pos):
        ang = pos[:, None] * inv[None, :]
        ang = jnp.concatenate([ang, ang], axis=-1)
        cos, sin = jnp.cos(ang), jnp.sin(ang)
        return cos, jnp.where(low, -sin, 0.0), jnp.where(low, 0.0, sin)

    zr = jnp.zeros((rows, axis_dim), jnp.float32)
    zc = jnp.zeros((GRID_W, axis_dim), jnp.float32)
    row_tabs = [jnp.concatenate([p, zr], axis=-1) for p in parts(r)]
    col_tabs = [jnp.concatenate([zc, p], axis=-1) for p in parts(cpos)]
    return row_tabs, col_tabs


def kernel(x, c, w_ada, b_ada, norm_mix_pre, norm_mix_post, w_in, q_norm, k_norm, hg_lower_bound,
           hg_out_norm, w_out, norm_ffn_pre, norm_ffn_post, w_up, conv_w, conv_b, w_down):
    batch, s, d = x.shape
    assert batch == 1 and d == D_MODEL and s % GRID_W == 0
    layer = 0
    x2 = x.reshape(s, d)
    tm_big = min(1024, s)
    tm_mid = min(512, s)
    tm_small = min(512, s)

    mod = _modulation(c.reshape(d, 1), w_ada[layer], b_ada[layer].reshape(1, -1))
    sh1, sc1, g1, sh2, sc2, g2 = [mod[:, k * d:(k + 1) * d] for k in range(6)]

    proj = _in_proj(x2, norm_mix_pre[layer].reshape(1, d), sc1, sh1,
                    w_in[layer].astype(jnp.bfloat16), tm_big)

    row_tabs, col_tabs = _rope_tables(s)
    qt, kt, vt = _att_prep(proj, row_tabs, col_tabs, q_norm[layer].reshape(1, -1),
                           k_norm[layer].reshape(1, -1), tm_mid)
    o_att = _attention(qt, kt, vt, tq=min(512, s), tk=min(512, s))

    rows = min(512, s)
    o_f, o_b = _hgrn(proj, hg_lower_bound[:, 0, :], hg_lower_bound[:, 1, :], rows)

    x1, h2 = _mix_out(o_att, o_f, o_b, proj, hg_out_norm[layer].reshape(1, -1),
                      w_out[layer].astype(jnp.bfloat16), x2, g1,
                      norm_mix_post[layer].reshape(1, d), norm_ffn_pre[layer].reshape(1, d),
                      sc2, sh2, tm_small)

    out = _ffn(h2, w_up[layer].astype(jnp.bfloat16), conv_w[layer], conv_b[layer].reshape(1, -1),
               w_down[layer].astype(jnp.bfloat16), x1, g2, norm_ffn_post[layer].reshape(1, d),
               tm_big, 512)
    return out.reshape(batch, s, d)
```

```python
import functools
import math

import numpy as np
import jax
import jax.numpy as jnp
from jax import lax
from jax.experimental import pallas as pl
from jax.experimental.pallas import tpu as pltpu

D_MODEL = 2048
GRID_W = 64
ATT_HEADS = 8
ATT_KV_HEADS = 2
ATT_GROUP = ATT_HEADS // ATT_KV_HEADS
HEAD_DIM = 128
ATT_WIDTH = ATT_HEADS * HEAD_DIM
KV_WIDTH = ATT_KV_HEADS * HEAD_DIM
ROPE_THETA = 10000.0
HG_WIDTH = 1024
HG_HEADS = 8
HG_HEAD_V = 128
HG_EXPAND = 128
IN_COLS = ATT_WIDTH + 2 * KV_WIDTH + 5 * HG_WIDTH
D_FF = 5632
EPS = 1e-6
LOG2E = math.log2(math.e)

LANES = 128
BF16_SUBLANES = 16
VMEM_LIMIT = 56 * 1024 * 1024
FFN_VMEM_LIMIT = 60 * 1024 * 1024

COL_TILE = 512
N_COL_TILES = IN_COLS // COL_TILE
HG_CHUNK = 128
HG_LEVELS = (64, 32, 16, 8, 4, 2, 1)
HG_HALF = 512
ATT_ROW_BLOCK = 128
NORM_ROW_BLOCK = 64
FFN_COL_GROUP = 512


def _params(*sem):
    return pltpu.CompilerParams(dimension_semantics=sem, vmem_limit_bytes=VMEM_LIMIT)


def _rms(x, w):
    return x * lax.rsqrt(jnp.mean(x * x, axis=-1, keepdims=True) + EPS) * w


def _sigmoid(x):
    return 1.0 / (1.0 + jnp.exp(-x))


def _silu(x):
    return x * _sigmoid(x)


def _mod_kernel(c_ref, w_ref, b_ref, o_ref):
    s = _silu(c_ref[...])
    o_ref[...] = jnp.sum(s * w_ref[...], axis=0, keepdims=True) + b_ref[...]


def _modulation(c_col, w_ada, b_ada):
    d, n = w_ada.shape
    tn = 1024
    return pl.pallas_call(
        _mod_kernel,
        grid=(n // tn,),
        in_specs=[pl.BlockSpec((d, 1), lambda j: (0, 0)),
                  pl.BlockSpec((d, tn), lambda j: (0, j)),
                  pl.BlockSpec((1, tn), lambda j: (0, j))],
        out_specs=pl.BlockSpec((1, tn), lambda j: (0, j)),
        out_shape=jax.ShapeDtypeStruct((1, n), jnp.float32),
        compiler_params=_params("arbitrary"),
        name="adaln_mod",
    )(c_col, w_ada, b_ada)


def _in_proj_kernel(x_ref, nw_ref, sc_ref, sh_ref, w_ref, o_ref, h_scr):
    @pl.when(pl.program_id(1) == 0)
    def _():
        for r0 in range(0, x_ref.shape[0], NORM_ROW_BLOCK):
            rows = slice(r0, r0 + NORM_ROW_BLOCK)
            h = _rms(x_ref[rows, :], nw_ref[...]) * (1.0 + sc_ref[...]) + sh_ref[...]
            h_scr[rows, :] = h.astype(jnp.bfloat16)

    o_ref[0] = jnp.dot(h_scr[...], w_ref[...], preferred_element_type=jnp.float32)


def _in_proj(x2, nw, sc, sh, w_in, tm):
    s, d = x2.shape
    vec = pl.BlockSpec((1, d), lambda i, j: (0, 0))
    return pl.pallas_call(
        _in_proj_kernel,
        grid=(s // tm, N_COL_TILES),
        in_specs=[pl.BlockSpec((tm, d), lambda i, j: (i, 0)), vec, vec, vec,
                  pl.BlockSpec((d, COL_TILE), lambda i, j: (0, j))],
        out_specs=pl.BlockSpec((1, tm, COL_TILE), lambda i, j: (j, i, 0)),
        out_shape=jax.ShapeDtypeStruct((N_COL_TILES, s, COL_TILE), jnp.float32),
        scratch_shapes=[pltpu.VMEM((tm, d), jnp.bfloat16)],
        compiler_params=_params("parallel", "arbitrary"),
        name="in_proj",
    )(x2, nw, sc, sh, w_in)


def _rope(y, cos, sin_a, sin_b):
    return (y * cos + pltpu.roll(y, LANES - 32, axis=1) * sin_a
            + pltpu.roll(y, 32, axis=1) * sin_b)


def _att_prep_kernel(q0_ref, q1_ref, kv_ref, cr_ref, ar_ref, br_ref, cc_ref, ac_ref, bc_ref,
                     qn_ref, kn_ref, q_out, k_out, v_out):
    def table(row_ref, col_ref):
        n = row_ref.shape[0]
        rows = jnp.concatenate([jnp.broadcast_to(row_ref[k:k + 1, :], (GRID_W, HEAD_DIM))
                                for k in range(n)], axis=0)
        return rows + jnp.concatenate([col_ref[...]] * n, axis=0)

    cos, sa, sb = table(cr_ref, cc_ref), table(ar_ref, ac_ref), table(br_ref, bc_ref)
    scale = HEAD_DIM ** -0.5 * LOG2E
    for t, src in enumerate((q0_ref, q1_ref)):
        for hh in range(COL_TILE // HEAD_DIM):
            xh = src[0, :, hh * HEAD_DIM:(hh + 1) * HEAD_DIM]
            y = _rope(_rms(xh, qn_ref[...]), cos, sa, sb) * scale
            col = t * COL_TILE + hh * HEAD_DIM
            q_out[:, col:col + HEAD_DIM] = y.astype(jnp.bfloat16)
    for hh in range(ATT_KV_HEADS):
        xh = kv_ref[0, :, hh * HEAD_DIM:(hh + 1) * HEAD_DIM]
        y = _rope(_rms(xh, kn_ref[...]), cos, sa, sb)
        k_out[:, hh * HEAD_DIM:(hh + 1) * HEAD_DIM] = y.astype(jnp.bfloat16)
    ones = jnp.ones((v_out.shape[0], HEAD_DIM), jnp.bfloat16)
    for hh in range(ATT_KV_HEADS):
        vh = kv_ref[0, :, KV_WIDTH + hh * HEAD_DIM:KV_WIDTH + (hh + 1) * HEAD_DIM]
        v_out[:, 2 * hh * HEAD_DIM:(2 * hh + 1) * HEAD_DIM] = vh.astype(jnp.bfloat16)
        v_out[:, (2 * hh + 1) * HEAD_DIM:(2 * hh + 2) * HEAD_DIM] = ones


def _att_prep(proj, row_tabs, col_tabs, qn, kn, tm):
    s = proj.shape[1]
    tile = lambda t: pl.BlockSpec((1, tm, COL_TILE), lambda i, t=t: (t, i, 0))
    rtab = pl.BlockSpec((tm // GRID_W, HEAD_DIM), lambda i: (i, 0))
    ctab = pl.BlockSpec((GRID_W, HEAD_DIM), lambda i: (0, 0))
    vec = pl.BlockSpec((1, HEAD_DIM), lambda i: (0, 0))
    return pl.pallas_call(
        _att_prep_kernel,
        grid=(s // tm,),
        in_specs=[tile(0), tile(1), tile(2), rtab, rtab, rtab, ctab, ctab, ctab, vec, vec],
        out_specs=[pl.BlockSpec((tm, ATT_WIDTH), lambda i: (i, 0)),
                   pl.BlockSpec((tm, KV_WIDTH), lambda i: (i, 0)),
                   pl.BlockSpec((tm, 2 * KV_WIDTH), lambda i: (i, 0))],
        out_shape=[jax.ShapeDtypeStruct((s, ATT_WIDTH), jnp.bfloat16),
                   jax.ShapeDtypeStruct((s, KV_WIDTH), jnp.bfloat16),
                   jax.ShapeDtypeStruct((s, 2 * KV_WIDTH), jnp.bfloat16)],
        compiler_params=_params("parallel"),
        name="att_prep",
    )(proj, proj, proj, *row_tabs, *col_tabs, qn, kn)


def _attention_kernel(q_ref, k_ref, v_ref, o_ref, m_scr, acc_scr, alpha_scr, s_scr, p_scr, *,
                      tq, tk):
    n_kv = k_ref.shape[0] // tk
    m_rows = ATT_GROUP * tq
    q = jnp.concatenate([q_ref[:, g * HEAD_DIM:(g + 1) * HEAD_DIM] for g in range(ATT_GROUP)],
                        axis=0)
    m_scr[...] = jnp.full(m_scr.shape, -jnp.inf, jnp.float32)
    acc_scr[...] = jnp.zeros(acc_scr.shape, jnp.float32)

    def scores(j):
        off = pl.multiple_of(j * tk, tk)
        return lax.dot_general(q, k_ref[pl.ds(off, tk), :], (((1,), (1,)), ((), ())),
                               preferred_element_type=jnp.float32)

    def softmax_pv(slot, j):
        for r0 in range(0, m_rows, ATT_ROW_BLOCK):
            rows = slice(r0, r0 + ATT_ROW_BLOCK)
            s = s_scr[slot, rows, :]
            m_old = m_scr[rows, :]
            m_new = jnp.maximum(m_old, jnp.max(s, axis=-1, keepdims=True))
            alpha_scr[rows, :] = jnp.exp2(m_old - m_new)
            m_scr[rows, :] = m_new
            p_scr[rows, :] = jnp.concatenate(
                [jnp.exp2(s[:, t * LANES:(t + 1) * LANES] - m_new) for t in range(tk // LANES)],
                axis=-1).astype(jnp.bfloat16)
        off = pl.multiple_of(j * tk, tk)
        pv = jnp.dot(p_scr[...], v_ref[pl.ds(off, tk), :], preferred_element_type=jnp.float32)
        alpha = alpha_scr[...]
        acc_scr[...] = jnp.concatenate([alpha, alpha], axis=-1) * acc_scr[...] + pv

    s_scr[0] = scores(0)

    def body(jj, carry):
        j = 2 * jj
        s_scr[1] = scores(j + 1)
        softmax_pv(0, j)
        s_scr[0] = scores(jnp.minimum(j + 2, n_kv - 1))
        softmax_pv(1, j + 1)
        return carry

    lax.fori_loop(0, n_kv // 2, body, 0)
    out = acc_scr[:, 0:HEAD_DIM] / acc_scr[:, HEAD_DIM:2 * HEAD_DIM]
    for g in range(ATT_GROUP):
        o_ref[:, g * HEAD_DIM:(g + 1) * HEAD_DIM] = out[g * tq:(g + 1) * tq].astype(o_ref.dtype)


def _attention(qt, kt, vt, tq, tk):
    s = qt.shape[0]
    gw = ATT_GROUP * HEAD_DIM
    return pl.pallas_call(
        functools.partial(_attention_kernel, tq=tq, tk=tk),
        grid=(ATT_KV_HEADS, s // tq),
        in_specs=[pl.BlockSpec((tq, gw), lambda h, i: (i, h)),
                  pl.BlockSpec((s, HEAD_DIM), lambda h, i: (0, h)),
                  pl.BlockSpec((s, 2 * HEAD_DIM), lambda h, i: (0, h))],
        out_specs=pl.BlockSpec((tq, gw), lambda h, i: (i, h)),
        out_shape=jax.ShapeDtypeStruct((s, ATT_WIDTH), jnp.bfloat16),
        scratch_shapes=[pltpu.VMEM((ATT_GROUP * tq, LANES), jnp.float32),
                        pltpu.VMEM((ATT_GROUP * tq, 2 * HEAD_DIM), jnp.float32),
                        pltpu.VMEM((ATT_GROUP * tq, LANES), jnp.float32),
                        pltpu.VMEM((2, ATT_GROUP * tq, tk), jnp.float32),
                        pltpu.VMEM((ATT_GROUP * tq, tk), jnp.bfloat16)],
        compiler_params=_params("parallel", "arbitrary"),
        name="attention",
    )(qt, kt, vt)


def _hgrn_tables(reverse):
    c = HG_CHUNK
    idx = np.arange(c)
    row, u = idx[:, None], idx[None, :]
    blocks = [(u >= row) if reverse else (u <= row), (u < row) if reverse else (u > row)]
    w = np.concatenate(blocks, axis=0).astype(np.float32)
    return jnp.asarray(np.tile(w, (1, 3)), jnp.bfloat16)


def _hgrn_level_masks(reverse):
    c = HG_CHUNK
    row = lax.broadcasted_iota(jnp.int32, (c, c), 0)
    col = lax.broadcasted_iota(jnp.int32, (c, c), 1)
    is_query, pair_mask = [], []
    for h in HG_LEVELS:
        q_row = (row % (2 * h) < h) if reverse else (row % (2 * h) >= h)
        q_col = (col % (2 * h) < h) if reverse else (col % (2 * h) >= h)
        same = (row // (2 * h)) == (col // (2 * h))
        pair_mask.append((same & q_row & jnp.logical_not(q_col)).astype(jnp.float32))
        is_query.append(q_row[:, 0:1])
    return is_query, pair_mask


def _hgrn_chunk(qf, kk, g, v, w3, is_query, pair_mask, state_ref, reverse):
    c = HG_CHUNK
    nt = (((1,), (1,)), ((), ()))
    g1 = g.astype(jnp.bfloat16)
    r1 = g - g1.astype(jnp.float32)
    g2 = r1.astype(jnp.bfloat16)
    g3 = (r1 - g2.astype(jnp.float32)).astype(jnp.bfloat16)
    sums = jnp.dot(w3, jnp.concatenate([g1, g2, g3], axis=0),
                   preferred_element_type=jnp.float32)
    b = sums[0:c]
    d_in = jnp.exp2(b)
    edge = d_in[0:1, :] if reverse else d_in[c - 1:c, :]
    q_in = (qf * d_in).astype(jnp.bfloat16)
    k_out = (kk * jnp.exp2(sums[c:2 * c])).astype(jnp.bfloat16)
    g_next = pltpu.roll(g, c - 1, axis=0)
    g_prev = pltpu.roll(g, 1, axis=0)
    pos4 = lax.broadcasted_iota(jnp.int32, (c, 1), 0) % 4
    sign = [jnp.where(q, 1.0, -1.0) for q in is_query]
    xs = []
    for l, h in enumerate(HG_LEVELS):
        if h >= 4:
            ref = h if reverse else h - 1
            b3 = b.reshape(c // (2 * h), 2 * h, b.shape[-1])
            bref = jnp.broadcast_to(b3[:, ref:ref + 1, :], b3.shape).reshape(b.shape)
            e_l = (b - bref) * sign[l]
        elif h == 2:
            zero = jnp.zeros_like(g)
            if reverse:
                e_l = jnp.where(pos4 == 0, g + g_next, jnp.where(pos4 == 1, g,
                                                                  jnp.where(pos4 == 2, zero, g_prev)))
            else:
                e_l = jnp.where(pos4 == 0, g_next, jnp.where(pos4 == 1, zero,
                                                             jnp.where(pos4 == 2, g, g + g_prev)))
        else:
            e_l = jnp.where(is_query[l], g, jnp.zeros_like(g))
        xs.append((jnp.where(is_query[l], qf, kk) * jnp.exp2(e_l)).astype(jnp.bfloat16))
    qk = qf * kk
    vb = v.astype(jnp.bfloat16)

    outs = []
    for h in range(HG_HALF // HG_EXPAND):
        sl = slice(h * HG_EXPAND, (h + 1) * HG_EXPAND)
        a = None
        for l in range(len(HG_LEVELS)):
            x = xs[l][:, sl]
            s_l = pair_mask[l] * lax.dot_general(x, x, nt, preferred_element_type=jnp.float32)
            a = s_l if a is None else a + s_l
        st = state_ref[h]
        oh = (jnp.dot(a.astype(jnp.bfloat16), vb[:, sl], preferred_element_type=jnp.float32)
              + lax.dot_general(q_in[:, sl], st.astype(jnp.bfloat16), nt,
                                preferred_element_type=jnp.float32)
              + jnp.sum(qk[:, sl], axis=-1, keepdims=True) * v[:, sl])
        outs.append(oh)
        upd = lax.dot_general(vb[:, sl], k_out[:, sl], (((0,), (0,)), ((), ())),
                              preferred_element_type=jnp.float32)
        state_ref[h] = st * edge[:, sl] + upd
    return jnp.concatenate(outs, axis=-1)


def _hgrn_kernel(qf_ref, qb_ref, ff_ref, fb_ref, if_ref, ib_ref, lbf_ref, lbb_ref, w3f_ref, w3b_ref,
                 of_ref, ob_ref, state_scr, *, rows):
    @pl.when(pl.program_id(1) == 0)
    def _():
        state_scr[...] = jnp.zeros(state_scr.shape, jnp.float32)

    def lower_bound(lb_ref):
        lbp = lb_ref[...]
        e = jnp.exp(lbp - jnp.max(lbp, axis=0, keepdims=True))
        return e[0:1, :] / jnp.sum(e, axis=0, keepdims=True)

    dirs = [(False, qf_ref, ff_ref, if_ref, of_ref, lower_bound(lbf_ref), w3f_ref[...]),
            (True, qb_ref, fb_ref, ib_ref, ob_ref, lower_bound(lbb_ref), w3b_ref[...])]
    masks = [_hgrn_level_masks(False), _hgrn_level_masks(True)]
    n_chunks = rows // HG_CHUNK
    for step in range(n_chunks):
        for d, (reverse, q_ref, f_ref, i_ref, o_ref, lb, w3) in enumerate(dirs):
            ci = n_chunks - 1 - step if reverse else step
            r = slice(ci * HG_CHUNK, (ci + 1) * HG_CHUNK)
            qf = _silu(q_ref[0, r, :]) * (HG_EXPAND ** -0.5)
            f = lb + (1.0 - lb) * _sigmoid(f_ref[0, r, :])
            o_ref[r, :] = _hgrn_chunk(qf, 1.0 - f, jnp.log2(f), i_ref[0, r, :], w3, masks[d][0],
                                      masks[d][1], state_scr.at[d], reverse)


def _hgrn(proj, lb_fwd, lb_bwd, rows):
    s = proj.shape[1]
    nblk = s // rows
    n_half = HG_WIDTH // HG_HALF
    w3f, w3b = _hgrn_tables(False), _hgrn_tables(True)
    fwd = lambda n: n
    bwd = lambda n: nblk - 1 - n

    def tile(t0, blk):
        return pl.BlockSpec((1, rows, COL_TILE), lambda h, n: (t0 + h, blk(n), 0))

    lb_spec = pl.BlockSpec((lb_fwd.shape[0], HG_HALF), lambda h, n: (0, h))
    w3_spec = pl.BlockSpec(w3f.shape, lambda h, n: (0, 0))
    out = jax.ShapeDtypeStruct((s, HG_WIDTH), jnp.float32)
    return pl.pallas_call(
        functools.partial(_hgrn_kernel, rows=rows),
        grid=(n_half, nblk),
        in_specs=[tile(3, fwd), tile(3, bwd), tile(5, fwd), tile(7, bwd), tile(9, fwd), tile(9, bwd),
                  lb_spec, lb_spec, w3_spec, w3_spec],
        out_specs=[pl.BlockSpec((rows, HG_HALF), lambda h, n: (fwd(n), h)),
                   pl.BlockSpec((rows, HG_HALF), lambda h, n: (bwd(n), h))],
        out_shape=[out, out],
        scratch_shapes=[pltpu.VMEM((2, HG_HALF // HG_EXPAND, HG_HEAD_V, HG_EXPAND), jnp.float32)],
        compiler_params=_params("parallel", "arbitrary"),
        name="hgrn",
    )(proj, proj, proj, proj, proj, proj, lb_fwd, lb_bwd, w3f, w3b)


def _mix_out_kernel(att_ref, of_ref, ob_ref, g0_ref, g1_ref, hn_ref, w_ref, x_ref, gate_ref,
                    npost_ref, npre_ref, sc_ref, sh_ref, x1_ref, h2_ref):
    hsum = of_ref[...] + ob_ref[...]
    gate = jnp.concatenate([g0_ref[0], g1_ref[0]], axis=-1)
    heads = []
    for h in range(HG_HEADS):
        sl = slice(h * HG_HEAD_V, (h + 1) * HG_HEAD_V)
        heads.append(_rms(hsum[:, sl], hn_ref[:, sl]) * _silu(gate[:, sl]))
    hg = jnp.concatenate(heads, axis=-1).astype(jnp.bfloat16)
    mix = (jnp.dot(att_ref[...], w_ref[0:ATT_WIDTH, :], preferred_element_type=jnp.float32)
           + jnp.dot(hg, w_ref[ATT_WIDTH:, :], preferred_element_type=jnp.float32))
    x1 = x_ref[...] + gate_ref[...] * _rms(mix, npost_ref[...])
    x1_ref[...] = x1
    h2 = _rms(x1, npre_ref[...]) * (1.0 + sc_ref[...]) + sh_ref[...]
    h2_ref[...] = h2.astype(jnp.bfloat16)


def _mix_out(o_att, o_f, o_b, proj, hn, w_out_bf16, x2, g1, npost, npre, sc2, sh2, tm):
    s, d = x2.shape
    vec = lambda n: pl.BlockSpec((1, n), lambda i: (0, 0))
    row = lambda n: pl.BlockSpec((tm, n), lambda i: (i, 0))
    gtile = lambda t: pl.BlockSpec((1, tm, COL_TILE), lambda i, t=t: (t, i, 0))
    return pl.pallas_call(
        _mix_out_kernel,
        grid=(s // tm,),
        in_specs=[row(ATT_WIDTH), row(HG_WIDTH), row(HG_WIDTH), gtile(11), gtile(12),
                  vec(HG_WIDTH),
                  pl.BlockSpec((d, d), lambda i: (0, 0), pipeline_mode=pl.Buffered(1)), row(d),
                  vec(d), vec(d), vec(d), vec(d), vec(d)],
        out_specs=[row(d), row(d)],
        out_shape=[jax.ShapeDtypeStruct((s, d), jnp.float32),
                   jax.ShapeDtypeStruct((s, d), jnp.bfloat16)],
        compiler_params=_params("parallel"),
        name="mix_out",
    )(o_att, o_f, o_b, proj, proj, hn, w_out_bf16, x2, g1, npost, npre, sc2, sh2)


def _gelu_tanh(x):
    return 0.5 * x * (1.0 + jnp.tanh(math.sqrt(2.0 / math.pi) * (x + 0.044715 * (x * x * x))))


def _ffn_kernel(h_ref, hp_ref, hn_ref, wa_ref, wb_ref, cwa_ref, cwb_ref, cba_ref, cbb_ref,
                wd_ref, x1_ref, gate_ref, npost_ref, o_ref, hext_scr, *, tm, tf):
    i, j = pl.program_id(0), pl.program_id(1)
    halo = BF16_SUBLANES
    n_ext = tm + 2 * halo

    @pl.when(j == 0)
    def _():
        hext_scr[0:halo, :] = jnp.where(i == 0, jnp.zeros_like(hp_ref[...]), hp_ref[...])
        hext_scr[halo:halo + tm, :] = h_ref[...]
        hext_scr[halo + tm:, :] = jnp.where(i == pl.num_programs(0) - 1,
                                            jnp.zeros_like(hn_ref[...]), hn_ref[...])
        o_ref[...] = jnp.zeros(o_ref.shape, o_ref.dtype)

    h_ext = hext_scr[...]

    def conv(u, cw, cb):
        u_prev = pltpu.roll(u, 1, axis=0)[halo:halo + tm]
        u_next = pltpu.roll(u, n_ext - 1, axis=0)[halo:halo + tm]
        return cw[0:1, :] * u_prev + cw[1:2, :] * u[halo:halo + tm] + cw[2:3, :] * u_next + cb

    groups = [slice(c0, c0 + FFN_COL_GROUP) for c0 in range(0, tf, FFN_COL_GROUP)]
    ups = [(jnp.dot(h_ext, wa_ref[:, cols], preferred_element_type=jnp.float32),
            jnp.dot(h_ext, wb_ref[:, cols], preferred_element_type=jnp.float32)) for cols in groups]
    for cols, (ua, ub) in zip(groups, ups):
        a = conv(ua, cwa_ref[:, cols], cba_ref[:, cols])
        b = conv(ub, cwb_ref[:, cols], cbb_ref[:, cols])
        act = (_gelu_tanh(a) * b).astype(jnp.bfloat16)
        o_ref[...] += jnp.dot(act, wd_ref[cols, :], preferred_element_type=jnp.float32)

    @pl.when(j == pl.num_programs(1) - 1)
    def _():
        for r0 in range(0, tm, NORM_ROW_BLOCK):
            rows = slice(r0, r0 + NORM_ROW_BLOCK)
            o_ref[rows, :] = x1_ref[rows, :] + gate_ref[...] * _rms(o_ref[rows, :], npost_ref[...])


def _ffn(h2, w_up_bf16, conv_w, conv_b, w_down_bf16, x1, g2, npost, tm, tf):
    s, d = x1.shape
    nf = D_FF // tf
    hb = tm // BF16_SUBLANES
    n_hblk = s // BF16_SUBLANES
    vec = pl.BlockSpec((1, d), lambda i, j: (0, 0))
    single = pl.Buffered(1)
    return pl.pallas_call(
        functools.partial(_ffn_kernel, tm=tm, tf=tf),
        grid=(s // tm, nf),
        in_specs=[pl.BlockSpec((tm, d), lambda i, j: (i, 0), pipeline_mode=single),
                  pl.BlockSpec((BF16_SUBLANES, d), lambda i, j: (jnp.maximum(i * hb - 1, 0), 0)),
                  pl.BlockSpec((BF16_SUBLANES, d),
                               lambda i, j: (jnp.minimum((i + 1) * hb, n_hblk - 1), 0)),
                  pl.BlockSpec((d, tf), lambda i, j: (0, j)),
                  pl.BlockSpec((d, tf), lambda i, j: (0, j + nf)),
                  pl.BlockSpec((3, tf), lambda i, j: (0, j)),
                  pl.BlockSpec((3, tf), lambda i, j: (0, j + nf)),
                  pl.BlockSpec((1, tf), lambda i, j: (0, j)),
                  pl.BlockSpec((1, tf), lambda i, j: (0, j + nf)),
                  pl.BlockSpec((tf, d), lambda i, j: (j, 0)),
                  pl.BlockSpec((tm, d), lambda i, j: (i, 0), pipeline_mode=single), vec, vec],
        out_specs=pl.BlockSpec((tm, d), lambda i, j: (i, 0)),
        out_shape=jax.ShapeDtypeStruct((s, d), jnp.float32),
        scratch_shapes=[pltpu.VMEM((tm + 2 * BF16_SUBLANES, d), jnp.bfloat16)],
        compiler_params=pltpu.CompilerParams(dimension_semantics=("parallel", "arbitrary"),
                                             vmem_limit_bytes=FFN_VMEM_LIMIT),
        name="conv_ffn",
    )(h2, h2, h2, w_up_bf16, w_up_bf16, conv_w, conv_w, conv_b, conv_b, w_down_bf16, x1, g2, npost)


def _rope_tables(s):
    rows = s // GRID_W
    axis_dim = HEAD_DIM // 2
    inv = ROPE_THETA ** (-(2.0 * jnp.arange(axis_dim // 2, dtype=jnp.float32)) / axis_dim)
    r = (jnp.arange(rows) - rows // 2).astype(jnp.float32)
    cpos = (jnp.arange(GRID_W) - GRID_W // 2).astype(jnp.float32)
    low = (jnp.arange(axis_dim) % axis_dim) < (axis_dim // 2)

    def parts(pos):
        ang = pos[:, None] * inv[None, :]
        ang = jnp.concatenate([ang, ang], axis=-1)
        cos, sin = jnp.cos(ang), jnp.sin(ang)
        return cos, jnp.where(low, -sin, 0.0), jnp.where(low, 0.0, sin)

    zr = jnp.zeros((rows, axis_dim), jnp.float32)
    zc = jnp.zeros((GRID_W, axis_dim), jnp.float32)
    row_tabs = [jnp.concatenate([p, zr], axis=-1) for p in parts(r)]
    col_tabs = [jnp.concatenate([zc, p], axis=-1) for p in parts(cpos)]
    return row_tabs, col_tabs


def kernel(x, c, w_ada, b_ada, norm_mix_pre, norm_mix_post, w_in, q_norm, k_norm, hg_lower_bound,
           hg_out_norm, w_out, norm_ffn_pre, norm_ffn_post, w_up, conv_w, conv_b, w_down):
    batch, s, d = x.shape
    assert batch == 1 and d == D_MODEL and s % GRID_W == 0
    layer = 0
    x2 = x.reshape(s, d)
    tm_big = min(1024, s)
    tm_small = min(512, s)

    mod = _modulation(c.reshape(d, 1), w_ada[layer], b_ada[layer].reshape(1, -1))
    sh1, sc1, g1, sh2, sc2, g2 = [mod[:, k * d:(k + 1) * d] for k in range(6)]

    proj = _in_proj(x2, norm_mix_pre[layer].reshape(1, d), sc1, sh1,
                    w_in[layer].astype(jnp.bfloat16), tm_big)

    row_tabs, col_tabs = _rope_tables(s)
    qt, kt, vt = _att_prep(proj, row_tabs, col_tabs, q_norm[layer].reshape(1, -1),
                           k_norm[layer].reshape(1, -1), tm_big)
    o_att = _attention(qt, kt, vt, tq=min(512, s), tk=min(512, s))

    rows = min(1024, s)
    o_f, o_b = _hgrn(proj, hg_lower_bound[:, 0, :], hg_lower_bound[:, 1, :], rows)

    x1, h2 = _mix_out(o_att, o_f, o_b, proj, hg_out_norm[layer].reshape(1, -1),
                      w_out[layer].astype(jnp.bfloat16), x2, g1,
                      norm_mix_post[layer].reshape(1, d), norm_ffn_pre[layer].reshape(1, d),
                      sc2, sh2, tm_small)

    out = _ffn(h2, w_up[layer].astype(jnp.bfloat16), conv_w[layer], conv_b[layer].reshape(1, -1),
               w_down[layer].astype(jnp.bfloat16), x1, g2, norm_ffn_post[layer].reshape(1, d),
               tm_big, 512)
    return out.reshape(batch, s, d)
```

```python
import functools
import math

import numpy as np
import jax
import jax.numpy as jnp
from jax import lax
from jax.experimental import pallas as pl
from jax.experimental.pallas import tpu as pltpu

D_MODEL = 2048
GRID_W = 64
ATT_HEADS = 8
ATT_KV_HEADS = 2
ATT_GROUP = ATT_HEADS // ATT_KV_HEADS
HEAD_DIM = 128
ATT_WIDTH = ATT_HEADS * HEAD_DIM
KV_WIDTH = ATT_KV_HEADS * HEAD_DIM
ROPE_THETA = 10000.0
HG_WIDTH = 1024
HG_HEADS = 8
HG_HEAD_V = 128
HG_EXPAND = 128
IN_COLS = ATT_WIDTH + 2 * KV_WIDTH + 5 * HG_WIDTH
D_FF = 5632
EPS = 1e-6
LOG2E = math.log2(math.e)

LANES = 128
BF16_SUBLANES = 16
VMEM_LIMIT = 56 * 1024 * 1024
BIG_TILE_VMEM_LIMIT = 60 * 1024 * 1024

COL_TILE = 512
N_COL_TILES = IN_COLS // COL_TILE
HG_CHUNK = 128
HG_LEVELS = (64, 32, 16, 8, 4, 2, 1)
HG_HALF = 512
ATT_ROW_BLOCK = 128
NORM_ROW_BLOCK = 64
FFN_COL_GROUP = 512


def _params(*sem):
    return pltpu.CompilerParams(dimension_semantics=sem, vmem_limit_bytes=VMEM_LIMIT)


def _rms(x, w):
    return x * lax.rsqrt(jnp.mean(x * x, axis=-1, keepdims=True) + EPS) * w


def _sigmoid(x):
    return 1.0 / (1.0 + jnp.exp(-x))


def _silu(x):
    return x * _sigmoid(x)


def _mod_kernel(c_ref, w_ref, b_ref, o_ref):
    s = _silu(c_ref[...])
    o_ref[...] = jnp.sum(s * w_ref[...], axis=0, keepdims=True) + b_ref[...]


def _modulation(c_col, w_ada, b_ada):
    d, n = w_ada.shape
    tn = 1024
    return pl.pallas_call(
        _mod_kernel,
        grid=(n // tn,),
        in_specs=[pl.BlockSpec((d, 1), lambda j: (0, 0)),
                  pl.BlockSpec((d, tn), lambda j: (0, j)),
                  pl.BlockSpec((1, tn), lambda j: (0, j))],
        out_specs=pl.BlockSpec((1, tn), lambda j: (0, j)),
        out_shape=jax.ShapeDtypeStruct((1, n), jnp.float32),
        compiler_params=_params("arbitrary"),
        name="adaln_mod",
    )(c_col, w_ada, b_ada)


def _in_proj_kernel(x_ref, nw_ref, sc_ref, sh_ref, w_ref, o_ref, h_scr):
    @pl.when(pl.program_id(1) == 0)
    def _():
        for r0 in range(0, x_ref.shape[0], NORM_ROW_BLOCK):
            rows = slice(r0, r0 + NORM_ROW_BLOCK)
            h = _rms(x_ref[rows, :], nw_ref[...]) * (1.0 + sc_ref[...]) + sh_ref[...]
            h_scr[rows, :] = h.astype(jnp.bfloat16)

    o_ref[0] = jnp.dot(h_scr[...], w_ref[...], preferred_element_type=jnp.float32)


def _in_proj(x2, nw, sc, sh, w_in, tm):
    s, d = x2.shape
    vec = pl.BlockSpec((1, d), lambda i, j: (0, 0))
    return pl.pallas_call(
        _in_proj_kernel,
        grid=(s // tm, N_COL_TILES),
        in_specs=[pl.BlockSpec((tm, d), lambda i, j: (i, 0)), vec, vec, vec,
                  pl.BlockSpec((d, COL_TILE), lambda i, j: (0, j))],
        out_specs=pl.BlockSpec((1, tm, COL_TILE), lambda i, j: (j, i, 0)),
        out_shape=jax.ShapeDtypeStruct((N_COL_TILES, s, COL_TILE), jnp.float32),
        scratch_shapes=[pltpu.VMEM((tm, d), jnp.bfloat16)],
        compiler_params=pltpu.CompilerParams(dimension_semantics=("parallel", "arbitrary"),
                                             vmem_limit_bytes=BIG_TILE_VMEM_LIMIT),
        name="in_proj",
    )(x2, nw, sc, sh, w_in)


def _rope(y, cos, sin_a, sin_b):
    return (y * cos + pltpu.roll(y, LANES - 32, axis=1) * sin_a
            + pltpu.roll(y, 32, axis=1) * sin_b)


def _att_prep_kernel(q0_ref, q1_ref, kv_ref, cr_ref, ar_ref, br_ref, cc_ref, ac_ref, bc_ref,
                     qn_ref, kn_ref, q_out, k_out, v_out):
    def table(row_ref, col_ref):
        n = row_ref.shape[0]
        rows = jnp.concatenate([jnp.broadcast_to(row_ref[k:k + 1, :], (GRID_W, HEAD_DIM))
                                for k in range(n)], axis=0)
        return rows + jnp.concatenate([col_ref[...]] * n, axis=0)

    cos, sa, sb = table(cr_ref, cc_ref), table(ar_ref, ac_ref), table(br_ref, bc_ref)
    scale = HEAD_DIM ** -0.5 * LOG2E
    for t, src in enumerate((q0_ref, q1_ref)):
        for hh in range(COL_TILE // HEAD_DIM):
            xh = src[0, :, hh * HEAD_DIM:(hh + 1) * HEAD_DIM]
            y = _rope(_rms(xh, qn_ref[...]), cos, sa, sb) * scale
            col = t * COL_TILE + hh * HEAD_DIM
            q_out[:, col:col + HEAD_DIM] = y.astype(jnp.bfloat16)
    for hh in range(ATT_KV_HEADS):
        xh = kv_ref[0, :, hh * HEAD_DIM:(hh + 1) * HEAD_DIM]
        y = _rope(_rms(xh, kn_ref[...]), cos, sa, sb)
        k_out[:, hh * HEAD_DIM:(hh + 1) * HEAD_DIM] = y.astype(jnp.bfloat16)
    ones = jnp.ones((v_out.shape[0], HEAD_DIM), jnp.bfloat16)
    for hh in range(ATT_KV_HEADS):
        vh = kv_ref[0, :, KV_WIDTH + hh * HEAD_DIM:KV_WIDTH + (hh + 1) * HEAD_DIM]
        v_out[:, 2 * hh * HEAD_DIM:(2 * hh + 1) * HEAD_DIM] = vh.astype(jnp.bfloat16)
        v_out[:, (2 * hh + 1) * HEAD_DIM:(2 * hh + 2) * HEAD_DIM] = ones


def _att_prep(proj, row_tabs, col_tabs, qn, kn, tm):
    s = proj.shape[1]
    tile = lambda t: pl.BlockSpec((1, tm, COL_TILE), lambda i, t=t: (t, i, 0))
    rtab = pl.BlockSpec((tm // GRID_W, HEAD_DIM), lambda i: (i, 0))
    ctab = pl.BlockSpec((GRID_W, HEAD_DIM), lambda i: (0, 0))
    vec = pl.BlockSpec((1, HEAD_DIM), lambda i: (0, 0))
    return pl.pallas_call(
        _att_prep_kernel,
        grid=(s // tm,),
        in_specs=[tile(0), tile(1), tile(2), rtab, rtab, rtab, ctab, ctab, ctab, vec, vec],
        out_specs=[pl.BlockSpec((tm, ATT_WIDTH), lambda i: (i, 0)),
                   pl.BlockSpec((tm, KV_WIDTH), lambda i: (i, 0)),
                   pl.BlockSpec((tm, 2 * KV_WIDTH), lambda i: (i, 0))],
        out_shape=[jax.ShapeDtypeStruct((s, ATT_WIDTH), jnp.bfloat16),
                   jax.ShapeDtypeStruct((s, KV_WIDTH), jnp.bfloat16),
                   jax.ShapeDtypeStruct((s, 2 * KV_WIDTH), jnp.bfloat16)],
        compiler_params=_params("parallel"),
        name="att_prep",
    )(proj, proj, proj, *row_tabs, *col_tabs, qn, kn)


def _attention_kernel(q_ref, k_ref, v_ref, o_ref, m_scr, acc_scr, alpha_scr, s_scr, p_scr, *,
                      tq, tk):
    n_kv = k_ref.shape[0] // tk
    m_rows = ATT_GROUP * tq
    q = jnp.concatenate([q_ref[:, g * HEAD_DIM:(g + 1) * HEAD_DIM] for g in range(ATT_GROUP)],
                        axis=0)
    m_scr[...] = jnp.full(m_scr.shape, -jnp.inf, jnp.float32)
    acc_scr[...] = jnp.zeros(acc_scr.shape, jnp.float32)

    def scores(j):
        off = pl.multiple_of(j * tk, tk)
        return lax.dot_general(q, k_ref[pl.ds(off, tk), :], (((1,), (1,)), ((), ())),
                               preferred_element_type=jnp.float32)

    def softmax_pv(slot, j):
        for r0 in range(0, m_rows, ATT_ROW_BLOCK):
            rows = slice(r0, r0 + ATT_ROW_BLOCK)
            s = s_scr[slot, rows, :]
            m_old = m_scr[rows, :]
            m_new = jnp.maximum(m_old, jnp.max(s, axis=-1, keepdims=True))
            alpha_scr[rows, :] = jnp.exp2(m_old - m_new)
            m_scr[rows, :] = m_new
            p_scr[rows, :] = jnp.concatenate(
                [jnp.exp2(s[:, t * LANES:(t + 1) * LANES] - m_new) for t in range(tk // LANES)],
                axis=-1).astype(jnp.bfloat16)
        off = pl.multiple_of(j * tk, tk)
        pv = jnp.dot(p_scr[...], v_ref[pl.ds(off, tk), :], preferred_element_type=jnp.float32)
        alpha = alpha_scr[...]
        acc_scr[...] = jnp.concatenate([alpha, alpha], axis=-1) * acc_scr[...] + pv

    s_scr[0] = scores(0)

    def body(jj, carry):
        j = 2 * jj
        s_scr[1] = scores(j + 1)
        softmax_pv(0, j)
        s_scr[0] = scores(jnp.minimum(j + 2, n_kv - 1))
        softmax_pv(1, j + 1)
        return carry

    lax.fori_loop(0, n_kv // 2, body, 0)
    out = acc_scr[:, 0:HEAD_DIM] / acc_scr[:, HEAD_DIM:2 * HEAD_DIM]
    for g in range(ATT_GROUP):
        o_ref[:, g * HEAD_DIM:(g + 1) * HEAD_DIM] = out[g * tq:(g + 1) * tq].astype(o_ref.dtype)


def _attention(qt, kt, vt, tq, tk):
    s = qt.shape[0]
    gw = ATT_GROUP * HEAD_DIM
    return pl.pallas_call(
        functools.partial(_attention_kernel, tq=tq, tk=tk),
        grid=(ATT_KV_HEADS, s // tq),
        in_specs=[pl.BlockSpec((tq, gw), lambda h, i: (i, h)),
                  pl.BlockSpec((s, HEAD_DIM), lambda h, i: (0, h)),
                  pl.BlockSpec((s, 2 * HEAD_DIM), lambda h, i: (0, h))],
        out_specs=pl.BlockSpec((tq, gw), lambda h, i: (i, h)),
        out_shape=jax.ShapeDtypeStruct((s, ATT_WIDTH), jnp.bfloat16),
        scratch_shapes=[pltpu.VMEM((ATT_GROUP * tq, LANES), jnp.float32),
                        pltpu.VMEM((ATT_GROUP * tq, 2 * HEAD_DIM), jnp.float32),
                        pltpu.VMEM((ATT_GROUP * tq, LANES), jnp.float32),
                        pltpu.VMEM((2, ATT_GROUP * tq, tk), jnp.float32),
                        pltpu.VMEM((ATT_GROUP * tq, tk), jnp.bfloat16)],
        compiler_params=_params("parallel", "arbitrary"),
        name="attention",
    )(qt, kt, vt)


def _hgrn_tables(reverse):
    c = HG_CHUNK
    idx = np.arange(c)
    row, u = idx[:, None], idx[None, :]
    blocks = [(u >= row) if reverse else (u <= row), (u < row) if reverse else (u > row)]
    w = np.concatenate(blocks, axis=0).astype(np.float32)
    return jnp.asarray(np.tile(w, (1, 3)), jnp.bfloat16)


def _hgrn_level_masks(reverse):
    c = HG_CHUNK
    row = lax.broadcasted_iota(jnp.int32, (c, c), 0)
    col = lax.broadcasted_iota(jnp.int32, (c, c), 1)
    is_query, pair_mask = [], []
    for h in HG_LEVELS:
        q_row = (row % (2 * h) < h) if reverse else (row % (2 * h) >= h)
        q_col = (col % (2 * h) < h) if reverse else (col % (2 * h) >= h)
        same = (row // (2 * h)) == (col // (2 * h))
        pair_mask.append((same & q_row & jnp.logical_not(q_col)).astype(jnp.float32))
        is_query.append(q_row[:, 0:1])
    return is_query, pair_mask


def _hgrn_chunk(qf, kk, g, v, w3, is_query, pair_mask, state_ref, reverse):
    c = HG_CHUNK
    nt = (((1,), (1,)), ((), ()))
    g1 = g.astype(jnp.bfloat16)
    r1 = g - g1.astype(jnp.float32)
    g2 = r1.astype(jnp.bfloat16)
    g3 = (r1 - g2.astype(jnp.float32)).astype(jnp.bfloat16)
    sums = jnp.dot(w3, jnp.concatenate([g1, g2, g3], axis=0),
                   preferred_element_type=jnp.float32)
    b = sums[0:c]
    d_in = jnp.exp2(b)
    edge = d_in[0:1, :] if reverse else d_in[c - 1:c, :]
    q_in = (qf * d_in).astype(jnp.bfloat16)
    k_out = (kk * jnp.exp2(sums[c:2 * c])).astype(jnp.bfloat16)
    g_next = pltpu.roll(g, c - 1, axis=0)
    g_prev = pltpu.roll(g, 1, axis=0)
    pos4 = lax.broadcasted_iota(jnp.int32, (c, 1), 0) % 4
    sign = [jnp.where(q, 1.0, -1.0) for q in is_query]
    xs = []
    for l, h in enumerate(HG_LEVELS):
        if h >= 4:
            ref = h if reverse else h - 1
            b3 = b.reshape(c // (2 * h), 2 * h, b.shape[-1])
            bref = jnp.broadcast_to(b3[:, ref:ref + 1, :], b3.shape).reshape(b.shape)
            e_l = (b - bref) * sign[l]
        elif h == 2:
            zero = jnp.zeros_like(g)
            if reverse:
                e_l = jnp.where(pos4 == 0, g + g_next, jnp.where(pos4 == 1, g,
                                                                  jnp.where(pos4 == 2, zero, g_prev)))
            else:
                e_l = jnp.where(pos4 == 0, g_next, jnp.where(pos4 == 1, zero,
                                                             jnp.where(pos4 == 2, g, g + g_prev)))
        else:
            e_l = jnp.where(is_query[l], g, jnp.zeros_like(g))
        xs.append((jnp.where(is_query[l], qf, kk) * jnp.exp2(e_l)).astype(jnp.bfloat16))
    qk = qf * kk
    vb = v.astype(jnp.bfloat16)

    outs = []
    for h in range(HG_HALF // HG_EXPAND):
        sl = slice(h * HG_EXPAND, (h + 1) * HG_EXPAND)
        a = None
        for l in range(len(HG_LEVELS)):
            x = xs[l][:, sl]
            s_l = pair_mask[l] * lax.dot_general(x, x, nt, preferred_element_type=jnp.float32)
            a = s_l if a is None else a + s_l
        st = state_ref[h]
        oh = (jnp.dot(a.astype(jnp.bfloat16), vb[:, sl], preferred_element_type=jnp.float32)
              + lax.dot_general(q_in[:, sl], st.astype(jnp.bfloat16), nt,
                                preferred_element_type=jnp.float32)
              + jnp.sum(qk[:, sl], axis=-1, keepdims=True) * v[:, sl])
        outs.append(oh)
        upd = lax.dot_general(vb[:, sl], k_out[:, sl], (((0,), (0,)), ((), ())),
                              preferred_element_type=jnp.float32)
        state_ref[h] = st * edge[:, sl] + upd
    return jnp.concatenate(outs, axis=-1)


def _hgrn_kernel(qf_ref, qb_ref, ff_ref, fb_ref, if_ref, ib_ref, lbf_ref, lbb_ref, w3f_ref, w3b_ref,
                 of_ref, ob_ref, state_scr, *, rows):
    @pl.when(pl.program_id(1) == 0)
    def _():
        state_scr[...] = jnp.zeros(state_scr.shape, jnp.float32)

    def lower_bound(lb_ref):
        lbp = lb_ref[...]
        e = jnp.exp(lbp - jnp.max(lbp, axis=0, keepdims=True))
        return e[0:1, :] / jnp.sum(e, axis=0, keepdims=True)

    dirs = [(False, qf_ref, ff_ref, if_ref, of_ref, lower_bound(lbf_ref), w3f_ref[...]),
            (True, qb_ref, fb_ref, ib_ref, ob_ref, lower_bound(lbb_ref), w3b_ref[...])]
    masks = [_hgrn_level_masks(False), _hgrn_level_masks(True)]
    n_chunks = rows // HG_CHUNK
    for step in range(n_chunks):
        for d, (reverse, q_ref, f_ref, i_ref, o_ref, lb, w3) in enumerate(dirs):
            ci = n_chunks - 1 - step if reverse else step
            r = slice(ci * HG_CHUNK, (ci + 1) * HG_CHUNK)
            qf = _silu(q_ref[0, r, :]) * (HG_EXPAND ** -0.5)
            f = lb + (1.0 - lb) * _sigmoid(f_ref[0, r, :])
            o_ref[r, :] = _hgrn_chunk(qf, 1.0 - f, jnp.log2(f), i_ref[0, r, :], w3, masks[d][0],
                                      masks[d][1], state_scr.at[d], reverse)


def _hgrn(proj, lb_fwd, lb_bwd, rows):
    s = proj.shape[1]
    nblk = s // rows
    n_half = HG_WIDTH // HG_HALF
    w3f, w3b = _hgrn_tables(False), _hgrn_tables(True)
    fwd = lambda n: n
    bwd = lambda n: nblk - 1 - n

    def tile(t0, blk):
        return pl.BlockSpec((1, rows, COL_TILE), lambda h, n: (t0 + h, blk(n), 0))

    lb_spec = pl.BlockSpec((lb_fwd.shape[0], HG_HALF), lambda h, n: (0, h))
    w3_spec = pl.BlockSpec(w3f.shape, lambda h, n: (0, 0))
    out = jax.ShapeDtypeStruct((s, HG_WIDTH), jnp.float32)
    return pl.pallas_call(
        functools.partial(_hgrn_kernel, rows=rows),
        grid=(n_half, nblk),
        in_specs=[tile(3, fwd), tile(3, bwd), tile(5, fwd), tile(7, bwd), tile(9, fwd), tile(9, bwd),
                  lb_spec, lb_spec, w3_spec, w3_spec],
        out_specs=[pl.BlockSpec((rows, HG_HALF), lambda h, n: (fwd(n), h)),
                   pl.BlockSpec((rows, HG_HALF), lambda h, n: (bwd(n), h))],
        out_shape=[out, out],
        scratch_shapes=[pltpu.VMEM((2, HG_HALF // HG_EXPAND, HG_HEAD_V, HG_EXPAND), jnp.float32)],
        compiler_params=_params("parallel", "arbitrary"),
        name="hgrn",
    )(proj, proj, proj, proj, proj, proj, lb_fwd, lb_bwd, w3f, w3b)


def _mix_out_kernel(att_ref, of_ref, ob_ref, g0_ref, g1_ref, hn_ref, w_ref, x_ref, gate_ref,
                    npost_ref, npre_ref, sc_ref, sh_ref, x1_ref, h2_ref):
    hsum = of_ref[...] + ob_ref[...]
    gate = jnp.concatenate([g0_ref[0], g1_ref[0]], axis=-1)
    heads = []
    for h in range(HG_HEADS):
        sl = slice(h * HG_HEAD_V, (h + 1) * HG_HEAD_V)
        heads.append(_rms(hsum[:, sl], hn_ref[:, sl]) * _silu(gate[:, sl]))
    hg = jnp.concatenate(heads, axis=-1).astype(jnp.bfloat16)
    mix = (jnp.dot(att_ref[...], w_ref[0:ATT_WIDTH, :], preferred_element_type=jnp.float32)
           + jnp.dot(hg, w_ref[ATT_WIDTH:, :], preferred_element_type=jnp.float32))
    x1 = x_ref[...] + gate_ref[...] * _rms(mix, npost_ref[...])
    x1_ref[...] = x1
    h2 = _rms(x1, npre_ref[...]) * (1.0 + sc_ref[...]) + sh_ref[...]
    h2_ref[...] = h2.astype(jnp.bfloat16)


def _mix_out(o_att, o_f, o_b, proj, hn, w_out_bf16, x2, g1, npost, npre, sc2, sh2, tm):
    s, d = x2.shape
    vec = lambda n: pl.BlockSpec((1, n), lambda i: (0, 0))
    row = lambda n: pl.BlockSpec((tm, n), lambda i: (i, 0))
    gtile = lambda t: pl.BlockSpec((1, tm, COL_TILE), lambda i, t=t: (t, i, 0))
    return pl.pallas_call(
        _mix_out_kernel,
        grid=(s // tm,),
        in_specs=[row(ATT_WIDTH), row(HG_WIDTH), row(HG_WIDTH), gtile(11), gtile(12),
                  vec(HG_WIDTH),
                  pl.BlockSpec((d, d), lambda i: (0, 0), pipeline_mode=pl.Buffered(1)), row(d),
                  vec(d), vec(d), vec(d), vec(d), vec(d)],
        out_specs=[row(d), row(d)],
        out_shape=[jax.ShapeDtypeStruct((s, d), jnp.float32),
                   jax.ShapeDtypeStruct((s, d), jnp.bfloat16)],
        compiler_params=_params("parallel"),
        name="mix_out",
    )(o_att, o_f, o_b, proj, proj, hn, w_out_bf16, x2, g1, npost, npre, sc2, sh2)


def _gelu_tanh(x):
    return 0.5 * x * (1.0 + jnp.tanh(math.sqrt(2.0 / math.pi) * (x + 0.044715 * (x * x * x))))


def _ffn_kernel(h_ref, hp_ref, hn_ref, wa_ref, wb_ref, cwa_ref, cwb_ref, cba_ref, cbb_ref,
                wd_ref, x1_ref, gate_ref, npost_ref, o_ref, hext_scr, *, tm, tf):
    i, j = pl.program_id(0), pl.program_id(1)
    halo = BF16_SUBLANES
    n_ext = tm + 2 * halo

    @pl.when(j == 0)
    def _():
        hext_scr[0:halo, :] = jnp.where(i == 0, jnp.zeros_like(hp_ref[...]), hp_ref[...])
        hext_scr[halo:halo + tm, :] = h_ref[...]
        hext_scr[halo + tm:, :] = jnp.where(i == pl.num_programs(0) - 1,
                                            jnp.zeros_like(hn_ref[...]), hn_ref[...])
        o_ref[...] = jnp.zeros(o_ref.shape, o_ref.dtype)

    h_ext = hext_scr[...]

    def conv(u, cw, cb):
        u_prev = pltpu.roll(u, 1, axis=0)[halo:halo + tm]
        u_next = pltpu.roll(u, n_ext - 1, axis=0)[halo:halo + tm]
        return cw[0:1, :] * u_prev + cw[1:2, :] * u[halo:halo + tm] + cw[2:3, :] * u_next + cb

    groups = [slice(c0, c0 + FFN_COL_GROUP) for c0 in range(0, tf, FFN_COL_GROUP)]
    ups = [(jnp.dot(h_ext, wa_ref[:, cols], preferred_element_type=jnp.float32),
            jnp.dot(h_ext, wb_ref[:, cols], preferred_element_type=jnp.float32)) for cols in groups]
    for cols, (ua, ub) in zip(groups, ups):
        a = conv(ua, cwa_ref[:, cols], cba_ref[:, cols])
        b = conv(ub, cwb_ref[:, cols], cbb_ref[:, cols])
        act = (_gelu_tanh(a) * b).astype(jnp.bfloat16)
        o_ref[...] += jnp.dot(act, wd_ref[cols, :], preferred_element_type=jnp.float32)

    @pl.when(j == pl.num_programs(1) - 1)
    def _():
        for r0 in range(0, tm, NORM_ROW_BLOCK):
            rows = slice(r0, r0 + NORM_ROW_BLOCK)
            o_ref[rows, :] = x1_ref[rows, :] + gate_ref[...] * _rms(o_ref[rows, :], npost_ref[...])


def _ffn(h2, w_up_bf16, conv_w, conv_b, w_down_bf16, x1, g2, npost, tm, tf):
    s, d = x1.shape
    nf = D_FF // tf
    hb = tm // BF16_SUBLANES
    n_hblk = s // BF16_SUBLANES
    vec = pl.BlockSpec((1, d), lambda i, j: (0, 0))
    single = pl.Buffered(1)
    return pl.pallas_call(
        functools.partial(_ffn_kernel, tm=tm, tf=tf),
        grid=(s // tm, nf),
        in_specs=[pl.BlockSpec((tm, d), lambda i, j: (i, 0), pipeline_mode=single),
                  pl.BlockSpec((BF16_SUBLANES, d), lambda i, j: (jnp.maximum(i * hb - 1, 0), 0)),
                  pl.BlockSpec((BF16_SUBLANES, d),
                               lambda i, j: (jnp.minimum((i + 1) * hb, n_hblk - 1), 0)),
                  pl.BlockSpec((d, tf), lambda i, j: (0, j)),
                  pl.BlockSpec((d, tf), lambda i, j: (0, j + nf)),
                  pl.BlockSpec((3, tf), lambda i, j: (0, j)),
                  pl.BlockSpec((3, tf), lambda i, j: (0, j + nf)),
                  pl.BlockSpec((1, tf), lambda i, j: (0, j)),
                  pl.BlockSpec((1, tf), lambda i, j: (0, j + nf)),
                  pl.BlockSpec((tf, d), lambda i, j: (j, 0)),
                  pl.BlockSpec((tm, d), lambda i, j: (i, 0), pipeline_mode=single), vec, vec],
        out_specs=pl.BlockSpec((tm, d), lambda i, j: (i, 0)),
        out_shape=jax.ShapeDtypeStruct((s, d), jnp.float32),
        scratch_shapes=[pltpu.VMEM((tm + 2 * BF16_SUBLANES, d), jnp.bfloat16)],
        compiler_params=pltpu.CompilerParams(dimension_semantics=("parallel", "arbitrary"),
                                             vmem_limit_bytes=BIG_TILE_VMEM_LIMIT),
        name="conv_ffn",
    )(h2, h2, h2, w_up_bf16, w_up_bf16, conv_w, conv_w, conv_b, conv_b, w_down_bf16, x1, g2, npost)


def _rope_tables(s):
    rows = s // GRID_W
    axis_dim = HEAD_DIM // 2
    inv = ROPE_THETA ** (-(2.0 * jnp.arange(axis_dim // 2, dtype=jnp.float32)) / axis_dim)
    r = (jnp.arange(rows) - rows // 2).astype(jnp.float32)
    cpos = (jnp.arange(GRID_W) - GRID_W // 2).astype(jnp.float32)
    low = (jnp.arange(axis_dim) % axis_dim) < (axis_dim // 2)

    def parts(pos):
        ang = pos[:, None] * inv[None, :]
        ang = jnp.concatenate([ang, ang], axis=-1)
        cos, sin = jnp.cos(ang), jnp.sin(ang)
        return cos, jnp.where(low, -sin, 0.0), jnp.where(low, 0.0, sin)

    zr = jnp.zeros((rows, axis_dim), jnp.float32)
    zc = jnp.zeros((GRID_W, axis_dim), jnp.float32)
    row_tabs = [jnp.concatenate([p, zr], axis=-1) for p in parts(r)]
    col_tabs = [jnp.concatenate([zc, p], axis=-1) for p in parts(cpos)]
    return row_tabs, col_tabs


def kernel(x, c, w_ada, b_ada, norm_mix_pre, norm_mix_post, w_in, q_norm, k_norm, hg_lower_bound,
           hg_out_norm, w_out, norm_ffn_pre, norm_ffn_post, w_up, conv_w, conv_b, w_down):
    batch, s, d = x.shape
    assert batch == 1 and d == D_MODEL and s % GRID_W == 0
    layer = 0
    x2 = x.reshape(s, d)
    tm_big = min(1024, s)
    tm_small = min(512, s)

    mod = _modulation(c.reshape(d, 1), w_ada[layer], b_ada[layer].reshape(1, -1))
    sh1, sc1, g1, sh2, sc2, g2 = [mod[:, k * d:(k + 1) * d] for k in range(6)]

    proj = _in_proj(x2, norm_mix_pre[layer].reshape(1, d), sc1, sh1,
                    w_in[layer].astype(jnp.bfloat16), min(2048, s))

    row_tabs, col_tabs = _rope_tables(s)
    qt, kt, vt = _att_prep(proj, row_tabs, col_tabs, q_norm[layer].reshape(1, -1),
                           k_norm[layer].reshape(1, -1), tm_big)
    o_att = _attention(qt, kt, vt, tq=min(512, s), tk=min(512, s))

    rows = min(1024, s)
    o_f, o_b = _hgrn(proj, hg_lower_bound[:, 0, :], hg_lower_bound[:, 1, :], rows)

    x1, h2 = _mix_out(o_att, o_f, o_b, proj, hg_out_norm[layer].reshape(1, -1),
                      w_out[layer].astype(jnp.bfloat16), x2, g1,
                      norm_mix_post[layer].reshape(1, d), norm_ffn_pre[layer].reshape(1, d),
                      sc2, sh2, tm_small)

    out = _ffn(h2, w_up[layer].astype(jnp.bfloat16), conv_w[layer], conv_b[layer].reshape(1, -1),
               w_down[layer].astype(jnp.bfloat16), x1, g2, norm_ffn_post[layer].reshape(1, d),
               tm_big, 512)
    return out.reshape(batch, s, d)
```
